```python
import math
import jax, jax.numpy as jnp
from jax import lax
import numpy as np

D_MODEL = 1024
BATCH = 4
SEQ = 8192
DEPTH = 1

HEAD_DIM = 64
SWA_HEADS = 8
SWA_KV_HEADS = 2
SWA_WINDOW = 128
SB_HEADS = 8
BLOCK = 128
SWA_Q = SWA_HEADS * HEAD_DIM
SWA_KV = SWA_KV_HEADS * HEAD_DIM
SB_W = SB_HEADS * HEAD_DIM
MIX_WIDTH = SWA_Q + SB_W
IN_COLS = SWA_Q + 2 * SWA_KV + 3 * SB_W
PEER_HEADS = 8
PEER_KEY_DIM = 256
PEER_N_KEYS = 128
PEER_N_EXPERTS = PEER_N_KEYS * PEER_N_KEYS
PEER_TOPK = 16
PEER_BLOCK = 128
DEEPNORM_ALPHA = (2.0 * DEPTH) ** 0.25
DEEPNORM_BETA = (8.0 * DEPTH) ** -0.25
LN_EPS = 1e-5

kernel_name = "hybrid_swa_stickbreak_peer_deepnorm_adaln"


def alibi_slopes(n_heads):
    return jnp.asarray([2.0 ** (-8.0 * (i + 1) / n_heads) for i in range(n_heads)], dtype=jnp.float32)


def layer_norm(x, g, b):
    xf = x.astype(jnp.float32)
    mu = jnp.mean(xf, axis=-1, keepdims=True)
    var = jnp.mean(jnp.square(xf - mu), axis=-1, keepdims=True)
    y = (xf - mu) * lax.rsqrt(var + LN_EPS) * g.astype(jnp.float32) + b.astype(jnp.float32)
    return y.astype(x.dtype)


def rms_norm(x, g):
    xf = x.astype(jnp.float32)
    y = xf * lax.rsqrt(jnp.mean(jnp.square(xf), axis=-1, keepdims=True) + LN_EPS) * g.astype(jnp.float32)
    return y.astype(x.dtype)


def swa_sink_alibi(q, k, v, sinks):
    b, s, _, d = q.shape
    nb = s // BLOCK
    r = SWA_HEADS // SWA_KV_HEADS
    qb = q.reshape(b, nb, BLOCK, SWA_KV_HEADS, r, d)

    def band(t):
        tb = t.reshape(b, nb, BLOCK, SWA_KV_HEADS, d)
        prev = jnp.concatenate([jnp.zeros_like(tb[:, :1]), tb[:, :-1]], axis=1)
        return jnp.concatenate([prev, tb], axis=2)

    kb, vb = band(k), band(v)
    scores = jnp.einsum('bnqgrd,bnkgd->bngrqk', qb, kb,
                        preferred_element_type=jnp.float32) / math.sqrt(d)
    qi = jnp.arange(BLOCK)
    kj = jnp.arange(2 * BLOCK)
    dist = (BLOCK + qi[:, None] - kj[None, :]).astype(jnp.float32)
    blk = jnp.arange(nb)
    valid = ((dist >= 0) & (dist < SWA_WINDOW))[None] & \
            ((blk[:, None, None] > 0) | (kj >= BLOCK)[None, None, :])
    slopes = alibi_slopes(SWA_HEADS).reshape(SWA_KV_HEADS, r)
    bias = -slopes[:, :, None, None] * dist[None, None]
    scores = jnp.where(valid[None, :, None, None], scores + bias[None, None], -jnp.inf)
    sink = sinks.astype(jnp.float32).reshape(SWA_KV_HEADS, r)[None, None, :, :, None, None]
    m = jnp.maximum(jnp.max(scores, axis=-1, keepdims=True), sink)
    p = jnp.exp(scores - m)
    den = jnp.sum(p, axis=-1, keepdims=True) + jnp.exp(sink - m)
    out = jnp.einsum('bngrqk,bnkgd->bnqgrd', (p / den).astype(v.dtype), vb)
    return out.reshape(b, s, SWA_Q)


def stick_breaking(q, k, v):
    b, s, h, d = q.shape
    outs = []
    for i in range(s // BLOCK):
        end = (i + 1) * BLOCK
        qb = q[:, i * BLOCK:end]
        kb = k[:, :end]
        vb = v[:, :end]
        z = jnp.einsum('bqhd,bkhd->bhqk', qb, kb,
                       preferred_element_type=jnp.float32) / math.sqrt(d)
        qpos = i * BLOCK + jnp.arange(BLOCK)
        kpos = jnp.arange(end)
        causal = kpos[None, :] < qpos[:, None]
        log_1m = jnp.where(causal, jax.nn.log_sigmoid(-z), 0.0)
        between = lax.cumsum(log_1m, axis=3, reverse=True) - log_1m
        a = jnp.where(causal, jnp.exp(jax.nn.log_sigmoid(z) + between), 0.0)
        outs.append(jnp.einsum('bhqk,bkhd->bqhd', a.astype(v.dtype), vb))
    return jnp.concatenate(outs, axis=1).reshape(b, s, h * d)


def peer(xm, w_q, sub_keys, u, v):
    b, s, dm = xm.shape
    q = jnp.einsum('bsd,dk->bsk', xm, w_q).reshape(b, s, PEER_HEADS, 2, PEER_KEY_DIM // 2)
    sc = jnp.einsum('bshpc,hpnc->bshpn', q, sub_keys, preferred_element_type=jnp.float32)
    top_s, top_i = lax.top_k(sc, PEER_TOPK)
    cand = top_s[..., 0, :, None] + top_s[..., 1, None, :]
    cand = cand.reshape(b, s, PEER_HEADS, PEER_TOPK * PEER_TOPK)
    best_s, best_c = lax.top_k(cand, PEER_TOPK)
    i1 = jnp.take_along_axis(top_i[..., 0, :], best_c // PEER_TOPK, axis=-1)
    i2 = jnp.take_along_axis(top_i[..., 1, :], best_c % PEER_TOPK, axis=-1)
    expert = i1 * PEER_N_KEYS + i2
    g = jax.nn.softmax(best_s, axis=-1)
    n_blk = (b * s) // PEER_BLOCK
    e_per_tok = PEER_HEADS * PEER_TOPK
    xs = xm.reshape(n_blk, PEER_BLOCK, dm)
    es = expert.reshape(n_blk, PEER_BLOCK, e_per_tok)
    gs = g.reshape(n_blk, PEER_BLOCK, e_per_tok)

    def block(args):
        xb, eb, gb = args
        ub = u[eb]
        hb = jax.nn.gelu(jnp.einsum('pd,ped->pe', xb, ub, preferred_element_type=jnp.float32))
        return jnp.einsum('pe,ped->pd', (gb * hb).astype(v.dtype), v[eb])

    y = lax.map(block, (xs, es, gs))
    return y.reshape(b, s, dm)


def setup_inputs(seed: int = 0) -> dict:
    key = jax.random.key(seed)
    ks = jax.random.split(key, 20)
    L, D = DEPTH, D_MODEL
    f32 = jnp.float32
    x = jax.random.normal(ks[0], (BATCH, SEQ, D), f32)
    c = jax.random.normal(ks[1], (BATCH, D), f32)
    w_ada = jax.random.normal(ks[2], (L, D, 6 * D), f32) * (0.1 * D ** -0.5)
    gate_pattern = jnp.concatenate([jnp.zeros((2 * D,), f32), jnp.ones((D,), f32)])
    b_ada = jnp.tile(gate_pattern, 2)[None] + 0.02 * jax.random.normal(ks[3], (L, 6 * D), f32)
    w_in = jax.random.normal(ks[4], (L, D, IN_COLS), f32) * D ** -0.5
    swa_sinks = 0.5 * jax.random.normal(ks[5], (L, SWA_HEADS), f32)
    group_norm_a = 1.0 + 0.02 * jax.random.normal(ks[6], (L, SWA_Q), f32)
    group_norm_b = 1.0 + 0.02 * jax.random.normal(ks[7], (L, SB_W), f32)
    w_out = jax.random.normal(ks[8], (L, MIX_WIDTH, D), f32) * (MIX_WIDTH ** -0.5 * DEEPNORM_BETA)
    ln1_g = 1.0 + 0.02 * jax.random.normal(ks[9], (L, D), f32)
    ln1_b = 0.02 * jax.random.normal(ks[10], (L, D), f32)
    peer_w_q = jax.random.normal(ks[11], (L, D, PEER_HEADS * PEER_KEY_DIM), f32) * D ** -0.5
    peer_sub_keys = jax.random.normal(ks[12], (L, PEER_HEADS, 2, PEER_N_KEYS, PEER_KEY_DIM // 2), f32) \
        * (PEER_KEY_DIM // 2) ** -0.5
    peer_u = jax.random.normal(ks[13], (L, PEER_N_EXPERTS, D), f32) * D ** -0.5
    peer_v = jax.random.normal(ks[14], (L, PEER_N_EXPERTS, D), f32) \
        * ((PEER_HEADS * PEER_TOPK) ** -0.5 * DEEPNORM_BETA)
    ln2_g = 1.0 + 0.02 * jax.random.normal(ks[15], (L, D), f32)
    ln2_b = 0.02 * jax.random.normal(ks[16], (L, D), f32)
    return {"x": x, "c": c, "w_ada": w_ada, "b_ada": b_ada, "w_in": w_in,
            "swa_sinks": swa_sinks, "group_norm_a": group_norm_a, "group_norm_b": group_norm_b,
            "w_out": w_out, "ln1_g": ln1_g, "ln1_b": ln1_b, "peer_w_q": peer_w_q,
            "peer_sub_keys": peer_sub_keys, "peer_u": peer_u, "peer_v": peer_v,
            "ln2_g": ln2_g, "ln2_b": ln2_b}


def reference(x, c, w_ada, b_ada, w_in, swa_sinks, group_norm_a, group_norm_b, w_out,
              ln1_g, ln1_b, peer_w_q, peer_sub_keys, peer_u, peer_v, ln2_g, ln2_b):
    b, s, _ = x.shape
    split_pts = [SWA_Q, SWA_Q + SWA_KV, SWA_Q + 2 * SWA_KV,
                 SWA_Q + 2 * SWA_KV + SB_W, SWA_Q + 2 * SWA_KV + 2 * SB_W]
    for l in range(DEPTH):
        ada = jnp.einsum('bd,de->be', jax.nn.silu(c), w_ada[l]) + b_ada[l]
        shift1, scale1, gate1, shift2, scale2, gate2 = [t[:, None, :] for t in jnp.split(ada, 6, axis=-1)]

        h = x * (1.0 + scale1) + shift1
        proj = jnp.einsum('bsd,dc->bsc', h, w_in[l])
        qa, ka, va, qb, kb, vb = jnp.split(proj, split_pts, axis=-1)
        oa = swa_sink_alibi(qa.reshape(b, s, SWA_HEADS, HEAD_DIM),
                            ka.reshape(b, s, SWA_KV_HEADS, HEAD_DIM),
                            va.reshape(b, s, SWA_KV_HEADS, HEAD_DIM), swa_sinks[l])
        ob = stick_breaking(qb.reshape(b, s, SB_HEADS, HEAD_DIM),
                            kb.reshape(b, s, SB_HEADS, HEAD_DIM),
                            vb.reshape(b, s, SB_HEADS, HEAD_DIM))
        o = jnp.concatenate([rms_norm(oa, group_norm_a[l]), rms_norm(ob, group_norm_b[l])], axis=-1)
        mix = jnp.einsum('bsc,cd->bsd', o, w_out[l])
        x = layer_norm(DEEPNORM_ALPHA * x + gate1 * mix, ln1_g[l], ln1_b[l])

        h = x * (1.0 + scale2) + shift2
        y = peer(h, peer_w_q[l], peer_sub_keys[l], peer_u[l], peer_v[l])
        x = layer_norm(DEEPNORM_ALPHA * x + gate2 * y, ln2_g[l], ln2_b[l])
    return x
```

```python
import functools
import math

import jax
import jax.numpy as jnp
from jax import lax
from jax.experimental import pallas as pl
from jax.experimental.pallas import tpu as pltpu

F32 = jnp.float32
BF16 = jnp.bfloat16
I32 = jnp.int32

HEAD_DIM = 64
SWA_HEADS = 8
SWA_KV_HEADS = 2
SWA_WINDOW = 128
SB_HEADS = 8
BLOCK = 128
SWA_Q = SWA_HEADS * HEAD_DIM
SWA_KV = SWA_KV_HEADS * HEAD_DIM
SB_W = SB_HEADS * HEAD_DIM
PEER_HEADS = 8
PEER_KEY_DIM = 256
PEER_N_KEYS = 128
PEER_TOPK = 16
PICKS = PEER_HEADS * PEER_TOPK
DEPTH = 1
DEEPNORM_ALPHA = (2.0 * DEPTH) ** 0.25
LN_EPS = 1e-5

LANES = 128
SUBLANES = 8
VMEM_LIMIT = 56 * 1024 * 1024


def _params(sem, vmem=None):
    return pltpu.CompilerParams(dimension_semantics=sem, vmem_limit_bytes=vmem)


def _dot_nt(a, b):
    return lax.dot_general(a, b, (((1,), (1,)), ((), ())), preferred_element_type=F32)


def _layer_norm(y, g, b):
    mu = jnp.mean(y, axis=-1, keepdims=True)
    d = y - mu
    var = jnp.mean(d * d, axis=-1, keepdims=True)
    return d * lax.rsqrt(var + LN_EPS) * g + b


def _ada_body(c_ref, w_ref, b_ref, o_ref):
    c = c_ref[...]
    a = c * jax.nn.sigmoid(c)
    o_ref[...] = jnp.dot(a, w_ref[...], preferred_element_type=F32,
                         precision=lax.Precision.HIGHEST) + b_ref[...]


def _ada(c, w, b):
    bsz, d = c.shape
    e = w.shape[1]
    tn = 1024
    return pl.pallas_call(
        _ada_body, grid=(e // tn,),
        in_specs=[pl.BlockSpec((bsz, d), lambda j: (0, 0)),
                  pl.BlockSpec((d, tn), lambda j: (0, j)),
                  pl.BlockSpec((1, tn), lambda j: (0, j))],
        out_specs=pl.BlockSpec((bsz, tn), lambda j: (0, j)),
        out_shape=jax.ShapeDtypeStruct((bsz, e), F32),
        compiler_params=_params(("arbitrary",)), name="ada",
    )(c, w, b)


_IN_SPLITS = (
    (0, SWA_HEADS, 0.125), (SWA_Q, SWA_KV_HEADS, 1.0), (SWA_Q + SWA_KV, SWA_KV_HEADS, 1.0),
    (SWA_Q + 2 * SWA_KV, SB_HEADS, 0.125), (SWA_Q + 2 * SWA_KV + SB_W, SB_HEADS, 1.0),
    (SWA_Q + 2 * SWA_KV + 2 * SB_W, SB_HEADS, 1.0))
IN_COLS = SWA_Q + 2 * SWA_KV + 3 * SB_W
_IN_CHUNK = 256


def _inproj_body(x_ref, sc_ref, sh_ref, w_ref, *out_refs):
    h = (x_ref[0] * (1.0 + sc_ref[0]) + sh_ref[0]).astype(BF16)
    for c in range(IN_COLS // _IN_CHUNK):
        r = jnp.dot(h, w_ref[:, c * _IN_CHUNK:(c + 1) * _IN_CHUNK], preferred_element_type=F32)
        for s in range(_IN_CHUNK // HEAD_DIM):
            col = c * _IN_CHUNK + s * HEAD_DIM
            for (start, heads, scale), o_ref in zip(_IN_SPLITS, out_refs):
                if start <= col < start + heads * HEAD_DIM:
                    piece = r[:, s * HEAD_DIM:(s + 1) * HEAD_DIM]
                    o_ref[0, (col - start) // HEAD_DIM] = (piece * scale).astype(BF16)


def _inproj(x, scale, shift, w_bf16, tm):
    b, s, d = x.shape
    outs = [jax.ShapeDtypeStruct((b, heads, s, HEAD_DIM), BF16) for _, heads, _ in _IN_SPLITS]
    out_specs = [pl.BlockSpec((1, heads, tm, HEAD_DIM), lambda bi, i: (bi, 0, i, 0))
                 for _, heads, _ in _IN_SPLITS]
    return pl.pallas_call(
        _inproj_body, grid=(b, s // tm),
        in_specs=[pl.BlockSpec((1, tm, d), lambda bi, i: (bi, i, 0)),
                  pl.BlockSpec((1, 1, d), lambda bi, i: (bi, 0, 0)),
                  pl.BlockSpec((1, 1, d), lambda bi, i: (bi, 0, 0)),
                  pl.BlockSpec((d, IN_COLS), lambda bi, i: (0, 0))],
        out_specs=out_specs, out_shape=outs,
        compiler_params=_params(("parallel", "parallel"), VMEM_LIMIT), name="inproj",
    )(x, scale, shift, w_bf16)


def _swa_body(sink_ref, q_ref, kp_ref, kc_ref, vp_ref, vc_ref, o_ref):
    j = pl.program_id(1)
    qi = lax.broadcasted_iota(I32, (BLOCK, BLOCK), 0)
    kj = lax.broadcasted_iota(I32, (BLOCK, BLOCK), 1)
    d_cur = (qi - kj).astype(F32)
    d_prev = d_cur + float(BLOCK)
    valid_cur = kj <= qi
    valid_prev = jnp.logical_and(kj > qi, j > 0)
    ratio = SWA_HEADS // SWA_KV_HEADS
    for g in range(SWA_KV_HEADS):
        kp, kc, vp, vc = kp_ref[0, g], kc_ref[0, g], vp_ref[0, g], vc_ref[0, g]
        for r in range(ratio):
            hd = g * ratio + r
            slope = 2.0 ** (-8.0 * (hd + 1) / SWA_HEADS)
            q = q_ref[0, hd]
            sp = jnp.where(valid_prev, _dot_nt(q, kp) - slope * d_prev, -jnp.inf)
            sc = jnp.where(valid_cur, _dot_nt(q, kc) - slope * d_cur, -jnp.inf)
            sink = sink_ref[hd]
            m = jnp.maximum(jnp.maximum(jnp.max(sp, axis=1, keepdims=True),
                                        jnp.max(sc, axis=1, keepdims=True)), sink)
            pp = jnp.exp(sp - m)
            pc = jnp.exp(sc - m)
            den = (jnp.sum(pp, axis=1, keepdims=True) + jnp.sum(pc, axis=1, keepdims=True)
                   + jnp.exp(sink - m))
            o = (jnp.dot((pp / den).astype(BF16), vp, preferred_element_type=F32)
                 + jnp.dot((pc / den).astype(BF16), vc, preferred_element_type=F32))
            o_ref[0, hd] = o


def _swa(qa, ka, va, sinks):
    b, _, s, _ = qa.shape
    nb = s // BLOCK
    cur = lambda bi, j: (bi, 0, j, 0)
    prev = lambda bi, j: (bi, 0, jnp.maximum(j - 1, 0), 0)
    kv_blk = (1, SWA_KV_HEADS, BLOCK, HEAD_DIM)
    return pl.pallas_call(
        _swa_body, grid=(b, nb),
        in_specs=[pl.BlockSpec(memory_space=pltpu.SMEM),
                  pl.BlockSpec((1, SWA_HEADS, BLOCK, HEAD_DIM), cur),
                  pl.BlockSpec(kv_blk, prev), pl.BlockSpec(kv_blk, cur),
                  pl.BlockSpec(kv_blk, prev), pl.BlockSpec(kv_blk, cur)],
        out_specs=pl.BlockSpec((1, SWA_HEADS, BLOCK, HEAD_DIM), cur),
        out_shape=jax.ShapeDtypeStruct((b, SWA_HEADS, s, HEAD_DIM), F32),
        compiler_params=_params(("parallel", "parallel")), name="swa",
    )(sinks, qa, ka, ka, va, va)


def _sb_body(q_ref, k_ref, v_ref, o_ref, *, t):
    i = pl.program_id(2)
    q = q_ref[0, 0]
    row = lax.broadcasted_iota(I32, (t, t), 0)
    col = lax.broadcasted_iota(I32, (t, t), 1)
    causal = col < row
    tri = (row > col).astype(BF16)
    tri2 = jnp.concatenate([tri, tri], axis=0)

    def tile(j, masked):
        off = pl.multiple_of(j * t, t)
        k = k_ref[0, 0, pl.ds(off, t), :]
        v = v_ref[0, 0, pl.ds(off, t), :]
        z = _dot_nt(q, k)
        l = -(jnp.maximum(z, 0.0) + jnp.log1p(jnp.exp(-jnp.abs(z))))
        lm = jnp.where(causal, l, 0.0) if masked else l
        hi = lm.astype(BF16)
        lo = (lm - hi.astype(F32)).astype(BF16)
        within = jnp.dot(jnp.concatenate([hi, lo], axis=1), tri2, preferred_element_type=F32)
        return z + l + within, jnp.sum(lm, axis=1, keepdims=True), v

    tot, lsum, v = tile(i, True)
    a = jnp.where(causal, jnp.exp(tot), 0.0)
    acc = jnp.dot(a.astype(BF16), v, preferred_element_type=F32)

    def body(jj, st):
        acc, carry = st
        tot, lsum, v = tile(i - 1 - jj, False)
        a = jnp.exp(tot + carry)
        return acc + jnp.dot(a.astype(BF16), v, preferred_element_type=F32), carry + lsum

    acc, _ = lax.fori_loop(0, i, body, (acc, lsum))
    o_ref[0, 0] = acc


def _sb(qb, kb, vb, t):
    b, h, s, _ = qb.shape
    full = pl.BlockSpec((1, 1, s, HEAD_DIM), lambda bi, hi, i: (bi, hi, 0, 0))
    tile = pl.BlockSpec((1, 1, t, HEAD_DIM), lambda bi, hi, i: (bi, hi, i, 0))
    return pl.pallas_call(
        functools.partial(_sb_body, t=t), grid=(b, h, s // t),
        in_specs=[tile, full, full], out_specs=tile,
        out_shape=jax.ShapeDtypeStruct((b, h, s, HEAD_DIM), F32),
        compiler_params=_params(("parallel", "parallel", "arbitrary"), VMEM_LIMIT), name="sb",
    )(qb, kb, vb)


def _outproj_body(oa_ref, ob_ref, x_ref, g1_ref, sc2_ref, sh2_ref, gn_ref, w_ref, lng_ref, lnb_ref,
                  wq_ref, x1_ref, h2_ref, q_ref):
    def group(o_ref, base, heads):
        ss = None
        for h in range(heads):
            o = o_ref[0, h]
            p = jnp.sum(o * o, axis=1, keepdims=True)
            ss = p if ss is None else ss + p
        inv = lax.rsqrt(ss / float(heads * HEAD_DIM) + LN_EPS)
        mix = None
        for h in range(heads):
            on = (o_ref[0, h] * inv * gn_ref[base + h]).astype(BF16)
            p = jnp.dot(on, w_ref[base + h], preferred_element_type=F32)
            mix = p if mix is None else mix + p
        return mix

    mix = group(oa_ref, 0, SWA_HEADS) + group(ob_ref, SWA_HEADS, SB_HEADS)
    x1 = _layer_norm(DEEPNORM_ALPHA * x_ref[0] + g1_ref[0] * mix, lng_ref[...], lnb_ref[...])
    x1_ref[0] = x1
    h2 = x1 * (1.0 + sc2_ref[0]) + sh2_ref[0]
    h2_ref[0] = h2
    q_ref[0] = jnp.dot(h2.astype(BF16), wq_ref[...], preferred_element_type=F32).astype(BF16)


def _outproj(oa, ob, x, gate1, scale2, shift2, gn, w_out, ln_g, ln_b, wq, tm):
    b, s, d = x.shape
    nq = wq.shape[1]
    nh = SWA_HEADS + SB_HEADS
    row = pl.BlockSpec((1, tm, d), lambda bi, i: (bi, i, 0))
    vec = pl.BlockSpec((1, 1, d), lambda bi, i: (bi, 0, 0))
    par = pl.BlockSpec((1, d), lambda bi, i: (0, 0))
    return pl.pallas_call(
        _outproj_body, grid=(b, s // tm),
        in_specs=[pl.BlockSpec((1, SWA_HEADS, tm, HEAD_DIM), lambda bi, i: (bi, 0, i, 0)),
                  pl.BlockSpec((1, SB_HEADS, tm, HEAD_DIM), lambda bi, i: (bi, 0, i, 0)),
                  row, vec, vec, vec,
                  pl.BlockSpec((nh, 1, HEAD_DIM), lambda bi, i: (0, 0, 0)),
                  pl.BlockSpec((nh, HEAD_DIM, d), lambda bi, i: (0, 0, 0)),
                  par, par,
                  pl.BlockSpec((d, nq), lambda bi, i: (0, 0))],
        out_specs=[row, row, pl.BlockSpec((1, tm, nq), lambda bi, i: (bi, i, 0))],
        out_shape=[jax.ShapeDtypeStruct((b, s, d), F32), jax.ShapeDtypeStruct((b, s, d), F32),
                   jax.ShapeDtypeStruct((b, s, nq), BF16)],
        compiler_params=_params(("parallel", "parallel"), VMEM_LIMIT), name="outproj",
    )(oa, ob, x, gate1, scale2, shift2, gn, w_out, ln_g, ln_b, wq)


def _top_rows(s, payload, k):
    n = s.shape[0]
    rows = lax.broadcasted_iota(I32, s.shape, 0)
    vals, pays = [], []
    for _ in range(k):
        m = jnp.max(s, axis=0, keepdims=True)
        first = jnp.min(jnp.where(s == m, rows, n), axis=0, keepdims=True)
        sel = rows == first
        vals.append(m)
        pays.append(jnp.max(jnp.where(sel, payload, -1), axis=0, keepdims=True))
        s = jnp.where(sel, -jnp.inf, s)
    return jnp.concatenate(vals, axis=0), jnp.concatenate(pays, axis=0)


def _topk_body(q_ref, sk_ref, e_ref, g_ref):
    t = q_ref.shape[0]
    key_ids = lax.broadcasted_iota(I32, (PEER_N_KEYS, t), 0)
    half = PEER_KEY_DIM // 2
    for h in range(PEER_HEADS):
        tops = []
        for p in range(2):
            qhp = q_ref[:, (2 * h + p) * half:(2 * h + p + 1) * half]
            sc = _dot_nt(sk_ref[2 * h + p], qhp)
            tops.append(_top_rows(sc, key_ids, PEER_TOPK))
        (s1, i1), (s2, i2) = tops
        cand = jnp.concatenate([s1[a:a + 1] + s2 for a in range(PEER_TOPK)], axis=0)
        expert = jnp.concatenate([i1[a:a + 1] * PEER_N_KEYS + i2 for a in range(PEER_TOPK)], axis=0)
        best, ids = _top_rows(cand, expert, PEER_TOPK)
        ex = jnp.exp(best - best[0:1])
        e_ref[h * PEER_TOPK:(h + 1) * PEER_TOPK, :] = ids
        g_ref[h * PEER_TOPK:(h + 1) * PEER_TOPK, :] = ex / jnp.sum(ex, axis=0, keepdims=True)


def _topk(q, sub_keys_bf16, t):
    n, nq = q.shape
    return pl.pallas_call(
        _topk_body, grid=(n // t,),
        in_specs=[pl.BlockSpec((t, nq), lambda i: (i, 0)),
                  pl.BlockSpec(sub_keys_bf16.shape, lambda i: (0, 0, 0))],
        out_specs=[pl.BlockSpec((PICKS, t), lambda i: (0, i)), pl.BlockSpec((PICKS, t), lambda i: (0, i))],
        out_shape=[jax.ShapeDtypeStruct((PICKS, n), I32), jax.ShapeDtypeStruct((PICKS, n), F32)],
        compiler_params=_params(("parallel",), VMEM_LIMIT), name="topk",
    )(q, sub_keys_bf16)


ROW_WORDS = 4


def _pack_table(tbl):
    e, d = tbl.shape
    tb = tbl.astype(BF16).reshape(e, d // (2 * LANES), 2, LANES).transpose(0, 1, 3, 2)
    return lax.bitcast_convert_type(tb, jnp.uint32).reshape(e * (d // (2 * LANES)), LANES)


def _expert_row(tbl_ref, e):
    w = tbl_ref[pl.ds(pl.multiple_of(e * ROW_WORDS, ROW_WORDS), ROW_WORDS), :]
    return pltpu.bitcast(w, BF16).astype(F32)


def _gelu_tanh(x):
    return 0.5 * x * (1.0 + jnp.tanh(math.sqrt(2.0 / math.pi) * (x + 0.044715 * (x * x * x))))


def _fold_pair(a, b, shift, sub):
    low = (sub & shift) == 0
    x = jnp.where(low, a, b)
    y = jnp.where(low, b, a)
    return x + pltpu.roll(y, shift, axis=0)


def _peer_u_body(idx_ref, x_ref, g_ref, tbl_ref, o_ref, h_scr, *, t):
    sub = lax.broadcasted_iota(I32, (SUBLANES, LANES), 0)
    ones = jnp.ones((SUBLANES, LANES), BF16)

    def token(tok, _):
        xt = x_ref[pl.ds(pl.multiple_of(tok * SUBLANES, SUBLANES), SUBLANES), :]
        groups = []
        for g0 in range(0, PICKS, SUBLANES):
            level = [_expert_row(tbl_ref, idx_ref[tok, g0 + k]) * xt for k in range(SUBLANES)]
            for shift in (1, 2, 4):
                level = [_fold_pair(level[2 * m], level[2 * m + 1], shift, sub) for m in range(len(level) // 2)]
            groups.append(level[0])
        part = jnp.concatenate(groups, axis=0)
        hi = part.astype(BF16)
        lo = (part - hi.astype(F32)).astype(BF16)
        hrow = _dot_nt(ones, hi) + _dot_nt(ones, lo)
        h_scr[pl.ds(tok, 1), :] = hrow[0:1, :]
        return 0

    lax.fori_loop(0, t, token, 0)
    o_ref[...] = g_ref[...] * _gelu_tanh(h_scr[...])


def _peer_u(idx, x8, gates, tbl, t):
    n = idx.shape[0]
    return pl.pallas_call(
        functools.partial(_peer_u_body, t=t), grid=(n // t,),
        in_specs=[pl.BlockSpec((t, PICKS), lambda i: (i, 0), memory_space=pltpu.SMEM),
                  pl.BlockSpec((t * SUBLANES, LANES), lambda i: (i, 0)),
                  pl.BlockSpec((t, PICKS), lambda i: (i, 0)),
                  pl.BlockSpec(tbl.shape, lambda i: (0, 0), pipeline_mode=pl.Buffered(1))],
        out_specs=pl.BlockSpec((t, PICKS), lambda i: (i, 0)),
        out_shape=jax.ShapeDtypeStruct((n, PICKS), F32),
        scratch_shapes=[pltpu.VMEM((t, PICKS), F32)],
        compiler_params=_params(("arbitrary",), VMEM_LIMIT), name="peer_u",
    )(idx, x8, gates, tbl)


def _peer_v_body(idx_ref, w_ref, tbl_ref, o_ref, *, t):
    n_acc = 4

    def token(tok, _):
        accs = [None] * n_acc
        for k in range(PICKS):
            term = w_ref[tok, k] * _expert_row(tbl_ref, idx_ref[tok, k])
            accs[k % n_acc] = term if accs[k % n_acc] is None else accs[k % n_acc] + term
        o_ref[pl.ds(pl.multiple_of(tok * SUBLANES, SUBLANES), SUBLANES), :] = (
            (accs[0] + accs[1]) + (accs[2] + accs[3]))
        return 0

    lax.fori_loop(0, t, token, 0)


def _peer_v(idx, wts, tbl, t):
    n = idx.shape[0]
    smem = pl.BlockSpec((t, PICKS), lambda i: (i, 0), memory_space=pltpu.SMEM)
    return pl.pallas_call(
        functools.partial(_peer_v_body, t=t), grid=(n // t,),
        in_specs=[smem, smem,
                  pl.BlockSpec(tbl.shape, lambda i: (0, 0), pipeline_mode=pl.Buffered(1))],
        out_specs=pl.BlockSpec((t * SUBLANES, LANES), lambda i: (i, 0)),
        out_shape=jax.ShapeDtypeStruct((n * SUBLANES, LANES), F32),
        compiler_params=_params(("arbitrary",), VMEM_LIMIT), name="peer_v",
    )(idx, wts, tbl)


def _final_body(x_ref, y_ref, g2_ref, lng_ref, lnb_ref, o_ref):
    o_ref[0] = _layer_norm(DEEPNORM_ALPHA * x_ref[0] + g2_ref[0] * y_ref[0], lng_ref[...], lnb_ref[...])


def _final(x1, y, gate2, ln_g, ln_b, tm):
    b, s, d = x1.shape
    row = pl.BlockSpec((1, tm, d), lambda bi, i: (bi, i, 0))
    par = pl.BlockSpec((1, d), lambda bi, i: (0, 0))
    return pl.pallas_call(
        _final_body, grid=(b, s // tm),
        in_specs=[row, row, pl.BlockSpec((1, 1, d), lambda bi, i: (bi, 0, 0)), par, par],
        out_specs=row, out_shape=jax.ShapeDtypeStruct((b, s, d), F32),
        compiler_params=_params(("parallel", "parallel")), name="final",
    )(x1, y, gate2, ln_g, ln_b)


def _tile(n, want):
    while n % want:
        want //= 2
    return want


def kernel(x, c, w_ada, b_ada, w_in, swa_sinks, group_norm_a, group_norm_b, w_out, ln1_g, ln1_b,
           peer_w_q, peer_sub_keys, peer_u, peer_v, ln2_g, ln2_b):
    b, s, d = x.shape
    n = b * s
    assert d == 8 * LANES and s % BLOCK == 0
    depth = w_ada.shape[0]
    tm = _tile(s, 512)
    for l in range(depth):
        ada = _ada(c, w_ada[l], b_ada[l][None])
        shift1, scale1, gate1, shift2, scale2, gate2 = [a[:, None, :] for a in jnp.split(ada, 6, axis=-1)]

        qa, ka, va, qb, kb, vb = _inproj(x, scale1, shift1, w_in[l].astype(BF16), tm)
        oa = _swa(qa, ka, va, swa_sinks[l])
        ob = _sb(qb, kb, vb, _tile(s, 256))

        gn = jnp.concatenate([group_norm_a[l], group_norm_b[l]]).reshape(-1, 1, HEAD_DIM)
        w_o = w_out[l].astype(BF16).reshape(-1, HEAD_DIM, d)
        x1, h2, q = _outproj(oa, ob, x, gate1, scale2, shift2, gn, w_o, ln1_g[l][None], ln1_b[l][None],
                             peer_w_q[l].astype(BF16), _tile(s, 256))

        sk = peer_sub_keys[l].astype(BF16).reshape(2 * PEER_HEADS, PEER_N_KEYS, PEER_KEY_DIM // 2)
        experts, gates = _topk(q.reshape(n, -1), sk, _tile(n, 128))
        idx = experts.T
        tp = _tile(n, 128)
        wts = _peer_u(idx, h2.reshape(n * SUBLANES, LANES), gates.T, _pack_table(peer_u[l]), tp)
        y = _peer_v(idx, wts, _pack_table(peer_v[l]), tp).reshape(b, s, d)

        x = _final(x1, y, gate2, ln2_g[l][None], ln2_b[l][None], tm)
    return x
```

```python
import functools
import math

import jax
import jax.numpy as jnp
from jax import lax
from jax.experimental import pallas as pl
from jax.experimental.pallas import tpu as pltpu

F32 = jnp.float32
BF16 = jnp.bfloat16
I32 = jnp.int32

HEAD_DIM = 64
SWA_HEADS = 8
SWA_KV_HEADS = 2
SWA_WINDOW = 128
SB_HEADS = 8
BLOCK = 128
SWA_Q = SWA_HEADS * HEAD_DIM
SWA_KV = SWA_KV_HEADS * HEAD_DIM
SB_W = SB_HEADS * HEAD_DIM
PEER_HEADS = 8
PEER_KEY_DIM = 256
PEER_N_KEYS = 128
PEER_TOPK = 16
PICKS = PEER_HEADS * PEER_TOPK
ROW_WORDS = 4
DEPTH = 1
DEEPNORM_ALPHA = (2.0 * DEPTH) ** 0.25
LN_EPS = 1e-5

LANES = 128
SUBLANES = 8
VMEM_LIMIT = 56 * 1024 * 1024


def _params(sem, vmem=None):
    return pltpu.CompilerParams(dimension_semantics=sem, vmem_limit_bytes=vmem)


def _dot_nt(a, b):
    return lax.dot_general(a, b, (((1,), (1,)), ((), ())), preferred_element_type=F32)


def _layer_norm(y, g, b):
    mu = jnp.mean(y, axis=-1, keepdims=True)
    d = y - mu
    var = jnp.mean(d * d, axis=-1, keepdims=True)
    return d * lax.rsqrt(var + LN_EPS) * g + b


def _ada_body(c_ref, w_ref, b_ref, o_ref):
    c = c_ref[...]
    a = c * jax.nn.sigmoid(c)
    o_ref[...] = jnp.dot(a, w_ref[...], preferred_element_type=F32,
                         precision=lax.Precision.HIGHEST) + b_ref[...]


def _ada(c, w, b):
    bsz, d = c.shape
    e = w.shape[1]
    tn = 1024
    return pl.pallas_call(
        _ada_body, grid=(e // tn,),
        in_specs=[pl.BlockSpec((bsz, d), lambda j: (0, 0)),
                  pl.BlockSpec((d, tn), lambda j: (0, j)),
                  pl.BlockSpec((1, tn), lambda j: (0, j))],
        out_specs=pl.BlockSpec((bsz, tn), lambda j: (0, j)),
        out_shape=jax.ShapeDtypeStruct((bsz, e), F32),
        compiler_params=_params(("arbitrary",)), name="ada",
    )(c, w, b)


_IN_SPLITS = (
    (0, SWA_HEADS, 0.125), (SWA_Q, SWA_KV_HEADS, 1.0), (SWA_Q + SWA_KV, SWA_KV_HEADS, 1.0),
    (SWA_Q + 2 * SWA_KV, SB_HEADS, 0.125), (SWA_Q + 2 * SWA_KV + SB_W, SB_HEADS, 1.0),
    (SWA_Q + 2 * SWA_KV + 2 * SB_W, SB_HEADS, 1.0))
IN_COLS = SWA_Q + 2 * SWA_KV + 3 * SB_W
_IN_CHUNK = 256


def _inproj_body(x_ref, sc_ref, sh_ref, w_ref, *out_refs):
    h = (x_ref[0] * (1.0 + sc_ref[0]) + sh_ref[0]).astype(BF16)
    for c in range(IN_COLS // _IN_CHUNK):
        r = jnp.dot(h, w_ref[:, c * _IN_CHUNK:(c + 1) * _IN_CHUNK], preferred_element_type=F32)
        for s in range(_IN_CHUNK // HEAD_DIM):
            col = c * _IN_CHUNK + s * HEAD_DIM
            for (start, heads, scale), o_ref in zip(_IN_SPLITS, out_refs):
                if start <= col < start + heads * HEAD_DIM:
                    piece = r[:, s * HEAD_DIM:(s + 1) * HEAD_DIM]
                    o_ref[0, (col - start) // HEAD_DIM] = (piece * scale).astype(BF16)


def _inproj(x, scale, shift, w_bf16, tm):
    b, s, d = x.shape
    outs = [jax.ShapeDtypeStruct((b, heads, s, HEAD_DIM), BF16) for _, heads, _ in _IN_SPLITS]
    out_specs = [pl.BlockSpec((1, heads, tm, HEAD_DIM), lambda bi, i: (bi, 0, i, 0))
                 for _, heads, _ in _IN_SPLITS]
    return pl.pallas_call(
        _inproj_body, grid=(b, s // tm),
        in_specs=[pl.BlockSpec((1, tm, d), lambda bi, i: (bi, i, 0)),
                  pl.BlockSpec((1, 1, d), lambda bi, i: (bi, 0, 0)),
                  pl.BlockSpec((1, 1, d), lambda bi, i: (bi, 0, 0)),
                  pl.BlockSpec((d, IN_COLS), lambda bi, i: (0, 0))],
        out_specs=out_specs, out_shape=outs,
        compiler_params=_params(("parallel", "parallel"), VMEM_LIMIT), name="inproj",
    )(x, scale, shift, w_bf16)


def _swa_body(sink_ref, q_ref, kp_ref, kc_ref, vp_ref, vc_ref, o_ref):
    j = pl.program_id(1)
    qi = lax.broadcasted_iota(I32, (BLOCK, BLOCK), 0)
    kj = lax.broadcasted_iota(I32, (BLOCK, BLOCK), 1)
    d_cur = (qi - kj).astype(F32)
    d_prev = d_cur + float(BLOCK)
    valid_cur = kj <= qi
    valid_prev = jnp.logical_and(kj > qi, j > 0)
    ratio = SWA_HEADS // SWA_KV_HEADS
    for g in range(SWA_KV_HEADS):
        kp, kc, vp, vc = kp_ref[0, g], kc_ref[0, g], vp_ref[0, g], vc_ref[0, g]
        for r in range(ratio):
            hd = g * ratio + r
            slope = 2.0 ** (-8.0 * (hd + 1) / SWA_HEADS)
            q = q_ref[0, hd]
            sp = jnp.where(valid_prev, _dot_nt(q, kp) - slope * d_prev, -jnp.inf)
            sc = jnp.where(valid_cur, _dot_nt(q, kc) - slope * d_cur, -jnp.inf)
            sink = sink_ref[hd]
            m = jnp.maximum(jnp.maximum(jnp.max(sp, axis=1, keepdims=True),
                                        jnp.max(sc, axis=1, keepdims=True)), sink)
            pp = jnp.exp(sp - m)
            pc = jnp.exp(sc - m)
            den = (jnp.sum(pp, axis=1, keepdims=True) + jnp.sum(pc, axis=1, keepdims=True)
                   + jnp.exp(sink - m))
            o = (jnp.dot((pp / den).astype(BF16), vp, preferred_element_type=F32)
                 + jnp.dot((pc / den).astype(BF16), vc, preferred_element_type=F32))
            o_ref[0, hd] = o


def _swa(qa, ka, va, sinks):
    b, _, s, _ = qa.shape
    nb = s // BLOCK
    cur = lambda bi, j: (bi, 0, j, 0)
    prev = lambda bi, j: (bi, 0, jnp.maximum(j - 1, 0), 0)
    kv_blk = (1, SWA_KV_HEADS, BLOCK, HEAD_DIM)
    return pl.pallas_call(
        _swa_body, grid=(b, nb),
        in_specs=[pl.BlockSpec(memory_space=pltpu.SMEM),
                  pl.BlockSpec((1, SWA_HEADS, BLOCK, HEAD_DIM), cur),
                  pl.BlockSpec(kv_blk, prev), pl.BlockSpec(kv_blk, cur),
                  pl.BlockSpec(kv_blk, prev), pl.BlockSpec(kv_blk, cur)],
        out_specs=pl.BlockSpec((1, SWA_HEADS, BLOCK, HEAD_DIM), cur),
        out_shape=jax.ShapeDtypeStruct((b, SWA_HEADS, s, HEAD_DIM), F32),
        compiler_params=_params(("parallel", "parallel")), name="swa",
    )(sinks, qa, ka, ka, va, va)


def _sb_body(q_ref, k_ref, v_ref, o_ref, *, t):
    i = pl.program_id(2)
    q = q_ref[0, 0]
    row = lax.broadcasted_iota(I32, (t, t), 0)
    col = lax.broadcasted_iota(I32, (t, t), 1)
    causal = col < row
    tri = (row > col).astype(BF16)
    tri2 = jnp.concatenate([tri, tri], axis=0)

    def tile(j, masked):
        off = pl.multiple_of(j * t, t)
        k = k_ref[0, 0, pl.ds(off, t), :]
        v = v_ref[0, 0, pl.ds(off, t), :]
        z = _dot_nt(q, k)
        l = -(jnp.maximum(z, 0.0) + jnp.log1p(jnp.exp(-jnp.abs(z))))
        lm = jnp.where(causal, l, 0.0) if masked else l
        hi = lm.astype(BF16)
        lo = (lm - hi.astype(F32)).astype(BF16)
        within = jnp.dot(jnp.concatenate([hi, lo], axis=1), tri2, preferred_element_type=F32)
        return z + l + within, jnp.sum(lm, axis=1, keepdims=True), v

    tot, lsum, v = tile(i, True)
    a = jnp.where(causal, jnp.exp(tot), 0.0)
    acc = jnp.dot(a.astype(BF16), v, preferred_element_type=F32)

    def body(jj, st):
        acc, carry = st
        tot, lsum, v = tile(i - 1 - jj, False)
        a = jnp.exp(tot + carry)
        return acc + jnp.dot(a.astype(BF16), v, preferred_element_type=F32), carry + lsum

    acc, _ = lax.fori_loop(0, i, body, (acc, lsum))
    o_ref[0, 0] = acc


def _sb(qb, kb, vb, t):
    b, h, s, _ = qb.shape
    full = pl.BlockSpec((1, 1, s, HEAD_DIM), lambda bi, hi, i: (bi, hi, 0, 0))
    tile = pl.BlockSpec((1, 1, t, HEAD_DIM), lambda bi, hi, i: (bi, hi, i, 0))
    return pl.pallas_call(
        functools.partial(_sb_body, t=t), grid=(b, h, s // t),
        in_specs=[tile, full, full], out_specs=tile,
        out_shape=jax.ShapeDtypeStruct((b, h, s, HEAD_DIM), F32),
        compiler_params=_params(("parallel", "parallel", "arbitrary"), VMEM_LIMIT), name="sb",
    )(qb, kb, vb)


def _outproj_body(oa_ref, ob_ref, x_ref, g1_ref, sc2_ref, sh2_ref, gn_ref, w_ref, lng_ref, lnb_ref,
                  wq_ref, x1_ref, h2_ref, q_ref):
    def group(o_ref, base, heads):
        ss = None
        for h in range(heads):
            o = o_ref[0, h]
            p = jnp.sum(o * o, axis=1, keepdims=True)
            ss = p if ss is None else ss + p
        inv = lax.rsqrt(ss / float(heads * HEAD_DIM) + LN_EPS)
        mix = None
        for h in range(heads):
            on = (o_ref[0, h] * inv * gn_ref[base + h]).astype(BF16)
            p = jnp.dot(on, w_ref[base + h], preferred_element_type=F32)
            mix = p if mix is None else mix + p
        return mix

    mix = group(oa_ref, 0, SWA_HEADS) + group(ob_ref, SWA_HEADS, SB_HEADS)
    x1 = _layer_norm(DEEPNORM_ALPHA * x_ref[0] + g1_ref[0] * mix, lng_ref[...], lnb_ref[...])
    x1_ref[0] = x1
    h2 = x1 * (1.0 + sc2_ref[0]) + sh2_ref[0]
    h2_ref[0] = h2
    q_ref[0] = jnp.dot(h2.astype(BF16), wq_ref[...], preferred_element_type=F32).astype(BF16)


def _outproj(oa, ob, x, gate1, scale2, shift2, gn, w_out, ln_g, ln_b, wq, tm):
    b, s, d = x.shape
    nq = wq.shape[1]
    nh = SWA_HEADS + SB_HEADS
    row = pl.BlockSpec((1, tm, d), lambda bi, i: (bi, i, 0))
    vec = pl.BlockSpec((1, 1, d), lambda bi, i: (bi, 0, 0))
    par = pl.BlockSpec((1, d), lambda bi, i: (0, 0))
    return pl.pallas_call(
        _outproj_body, grid=(b, s // tm),
        in_specs=[pl.BlockSpec((1, SWA_HEADS, tm, HEAD_DIM), lambda bi, i: (bi, 0, i, 0)),
                  pl.BlockSpec((1, SB_HEADS, tm, HEAD_DIM), lambda bi, i: (bi, 0, i, 0)),
                  row, vec, vec, vec,
                  pl.BlockSpec((nh, 1, HEAD_DIM), lambda bi, i: (0, 0, 0)),
                  pl.BlockSpec((nh, HEAD_DIM, d), lambda bi, i: (0, 0, 0)),
                  par, par,
                  pl.BlockSpec((d, nq), lambda bi, i: (0, 0))],
        out_specs=[row, row, pl.BlockSpec((1, tm, nq), lambda bi, i: (bi, i, 0))],
        out_shape=[jax.ShapeDtypeStruct((b, s, d), F32), jax.ShapeDtypeStruct((b, s, d), F32),
                   jax.ShapeDtypeStruct((b, s, nq), BF16)],
        compiler_params=_params(("parallel", "parallel"), VMEM_LIMIT), name="outproj",
    )(oa, ob, x, gate1, scale2, shift2, gn, w_out, ln_g, ln_b, wq)


def _top_rows(s, payload, k):
    n = s.shape[0]
    rows = lax.broadcasted_iota(I32, s.shape, 0)
    vals, pays = [], []
    for _ in range(k):
        m = jnp.max(s, axis=0, keepdims=True)
        first = jnp.min(jnp.where(s == m, rows, n), axis=0, keepdims=True)
        sel = rows == first
        vals.append(m)
        pays.append(jnp.max(jnp.where(sel, payload, -1), axis=0, keepdims=True))
        s = jnp.where(sel, -jnp.inf, s)
    return jnp.concatenate(vals, axis=0), jnp.concatenate(pays, axis=0)


def _topk_body(q_ref, sk_ref, e_ref, g_ref):
    t = q_ref.shape[0]
    key_ids = lax.broadcasted_iota(I32, (PEER_N_KEYS, t), 0)
    half = PEER_KEY_DIM // 2
    for h in range(PEER_HEADS):
        tops = []
        for p in range(2):
            qhp = q_ref[:, (2 * h + p) * half:(2 * h + p + 1) * half]
            sc = _dot_nt(sk_ref[2 * h + p], qhp)
            tops.append(_top_rows(sc, key_ids, PEER_TOPK))
        (s1, i1), (s2, i2) = tops
        cand = jnp.concatenate([s1[a:a + 1] + s2 for a in range(PEER_TOPK)], axis=0)
        expert = jnp.concatenate([i1[a:a + 1] * PEER_N_KEYS + i2 for a in range(PEER_TOPK)], axis=0)
        best, ids = _top_rows(cand, expert, PEER_TOPK)
        ex = jnp.exp(best - best[0:1])
        e_ref[h * PEER_TOPK:(h + 1) * PEER_TOPK, :] = ids * ROW_WORDS
        g_ref[h * PEER_TOPK:(h + 1) * PEER_TOPK, :] = ex / jnp.sum(ex, axis=0, keepdims=True)


def _topk(q, sub_keys_bf16, t):
    n, nq = q.shape
    return pl.pallas_call(
        _topk_body, grid=(n // t,),
        in_specs=[pl.BlockSpec((t, nq), lambda i: (i, 0)),
                  pl.BlockSpec(sub_keys_bf16.shape, lambda i: (0, 0, 0))],
        out_specs=[pl.BlockSpec((PICKS, t), lambda i: (0, i)), pl.BlockSpec((PICKS, t), lambda i: (0, i))],
        out_shape=[jax.ShapeDtypeStruct((PICKS, n), I32), jax.ShapeDtypeStruct((PICKS, n), F32)],
        compiler_params=_params(("parallel",), VMEM_LIMIT), name="topk",
    )(q, sub_keys_bf16)


def _pack_table(tbl):
    e, d = tbl.shape
    tb = tbl.astype(BF16).reshape(e, d // (2 * LANES), 2, LANES).transpose(0, 1, 3, 2)
    return lax.bitcast_convert_type(tb, jnp.uint32).reshape(e * (d // (2 * LANES)), LANES)


def _gelu_tanh(x):
    return 0.5 * x * (1.0 + jnp.tanh(math.sqrt(2.0 / math.pi) * (x + 0.044715 * (x * x * x))))


SLAB_ROWS = PICKS * ROW_WORDS
SLAB_COLS = PICKS * SUBLANES
TOKENS_PER_STEP = 4


def _expand_matrix():
    k = lax.broadcasted_iota(I32, (PICKS, SLAB_COLS), 0)
    c = lax.broadcasted_iota(I32, (PICKS, SLAB_COLS), 1)
    return (c // SUBLANES == k).astype(BF16)


def _gather_slab(idx_ref, tbl_ref, slab_ref, tok):
    rows = idx_ref.at[tok]
    for k in range(PICKS):
        src = pl.multiple_of(rows[k], ROW_WORDS)
        slab_ref[k * ROW_WORDS:(k + 1) * ROW_WORDS, :] = tbl_ref[pl.ds(src, ROW_WORDS), :]


def _split_rows(a):
    hi = a.astype(BF16).astype(F32)
    return jnp.concatenate([hi, a - hi], axis=0).astype(BF16)


def _chunk_diag():
    sub = lax.broadcasted_iota(I32, (SUBLANES, SLAB_COLS), 0)
    col = lax.broadcasted_iota(I32, (SUBLANES, SLAB_COLS), 1)
    return (col & (SUBLANES - 1)) == sub


def _pipelined_tokens(t, idx_ref, tbl_ref, slabs, compute_group):
    sets = (slabs[:TOKENS_PER_STEP], slabs[TOKENS_PER_STEP:])

    def gather_group(g, slab_set):
        for s, slab in enumerate(slab_set):
            _gather_slab(idx_ref, tbl_ref, slab, jnp.minimum(g * TOKENS_PER_STEP + s, t - 1))

    gather_group(0, sets[0])

    def step(p, _):
        gather_group(2 * p + 1, sets[1])
        compute_group(pl.multiple_of(2 * p * TOKENS_PER_STEP, TOKENS_PER_STEP), sets[0])
        gather_group(2 * p + 2, sets[0])
        compute_group(pl.multiple_of((2 * p + 1) * TOKENS_PER_STEP, TOKENS_PER_STEP), sets[1])
        return 0

    lax.fori_loop(0, t // (2 * TOKENS_PER_STEP), step, 0)


def _peer_u_body(idx_ref, x_ref, g_ref, tbl_ref, grp_ref, o_ref, h_scr, *slabs, t):
    diag = _chunk_diag()

    def compute_group(tok0, slab_set):
        parts = []
        for s, slab in enumerate(slab_set):
            xt = x_ref[pl.ds(pl.multiple_of((tok0 + s) * SUBLANES, SUBLANES), SUBLANES), :]
            r = _dot_nt(_split_rows(xt), pltpu.bitcast(slab[...], BF16))
            parts.append(_split_rows(jnp.where(diag, r[:SUBLANES] + r[SUBLANES:], 0.0)))
        hh = jnp.dot(jnp.concatenate(parts, axis=0), grp_ref[...], preferred_element_type=F32)
        rows = [jnp.sum(hh[2 * SUBLANES * s:2 * SUBLANES * (s + 1)], axis=0, keepdims=True)
                for s in range(len(slab_set))]
        h_scr[pl.ds(tok0, len(slab_set)), :] = jnp.concatenate(rows, axis=0)

    _pipelined_tokens(t, idx_ref, tbl_ref, slabs, compute_group)
    o_ref[...] = g_ref[...] * _gelu_tanh(h_scr[...])


def _peer_u(idx, x8, gates, tbl, t):
    n = idx.shape[0]
    grp = _expand_matrix().T
    return pl.pallas_call(
        functools.partial(_peer_u_body, t=t), grid=(n // t,),
        in_specs=[pl.BlockSpec((t, PICKS), lambda i: (i, 0), memory_space=pltpu.SMEM),
                  pl.BlockSpec((t * SUBLANES, LANES), lambda i: (i, 0)),
                  pl.BlockSpec((t, PICKS), lambda i: (i, 0)),
                  pl.BlockSpec(tbl.shape, lambda i: (0, 0), pipeline_mode=pl.Buffered(1)),
                  pl.BlockSpec(grp.shape, lambda i: (0, 0), pipeline_mode=pl.Buffered(1))],
        out_specs=pl.BlockSpec((t, PICKS), lambda i: (i, 0)),
        out_shape=jax.ShapeDtypeStruct((n, PICKS), F32),
        scratch_shapes=[pltpu.VMEM((t, PICKS), F32)]
        + [pltpu.VMEM((SLAB_ROWS, LANES), jnp.uint32)] * (2 * TOKENS_PER_STEP),
        compiler_params=_params(("arbitrary",), VMEM_LIMIT), name="peer_u",
    )(idx, x8, gates, tbl, grp)


def _peer_v_body(idx_ref, w_ref, tbl_ref, exp_ref, o_ref, rep_hi, rep_lo, *slabs, t):
    diag = _chunk_diag()
    w = w_ref[...]
    w_hi = w.astype(BF16)
    w_lo = (w - w_hi.astype(F32)).astype(BF16)
    rep_hi[...] = jnp.dot(w_hi, exp_ref[...], preferred_element_type=F32)
    rep_lo[...] = jnp.dot(w_lo, exp_ref[...], preferred_element_type=F32)

    def compute_group(tok0, slab_set):
        for s, slab in enumerate(slab_set):
            tok = tok0 + s
            lhs = jnp.concatenate([jnp.where(diag, rep_hi[pl.ds(tok, 1), :], 0.0),
                                   jnp.where(diag, rep_lo[pl.ds(tok, 1), :], 0.0)], axis=0).astype(BF16)
            out = jnp.dot(lhs, pltpu.bitcast(slab[...], BF16), preferred_element_type=F32)
            o_ref[pl.ds(pl.multiple_of(tok * SUBLANES, SUBLANES), SUBLANES), :] = (
                out[:SUBLANES] + out[SUBLANES:])

    _pipelined_tokens(t, idx_ref, tbl_ref, slabs, compute_group)


def _peer_v(idx, wts, tbl, t):
    n = idx.shape[0]
    exp = _expand_matrix()
    return pl.pallas_call(
        functools.partial(_peer_v_body, t=t), grid=(n // t,),
        in_specs=[pl.BlockSpec((t, PICKS), lambda i: (i, 0), memory_space=pltpu.SMEM),
                  pl.BlockSpec((t, PICKS), lambda i: (i, 0)),
                  pl.BlockSpec(tbl.shape, lambda i: (0, 0), pipeline_mode=pl.Buffered(1)),
                  pl.BlockSpec(exp.shape, lambda i: (0, 0), pipeline_mode=pl.Buffered(1))],
        out_specs=pl.BlockSpec((t * SUBLANES, LANES), lambda i: (i, 0)),
        out_shape=jax.ShapeDtypeStruct((n * SUBLANES, LANES), F32),
        scratch_shapes=[pltpu.VMEM((t, SLAB_COLS), F32)] * 2
        + [pltpu.VMEM((SLAB_ROWS, LANES), jnp.uint32)] * (2 * TOKENS_PER_STEP),
        compiler_params=_params(("arbitrary",), VMEM_LIMIT), name="peer_v",
    )(idx, wts, tbl, exp)


def _final_body(x_ref, y_ref, g2_ref, lng_ref, lnb_ref, o_ref):
    o_ref[0] = _layer_norm(DEEPNORM_ALPHA * x_ref[0] + g2_ref[0] * y_ref[0], lng_ref[...], lnb_ref[...])


def _final(x1, y, gate2, ln_g, ln_b, tm):
    b, s, d = x1.shape
    row = pl.BlockSpec((1, tm, d), lambda bi, i: (bi, i, 0))
    par = pl.BlockSpec((1, d), lambda bi, i: (0, 0))
    return pl.pallas_call(
        _final_body, grid=(b, s // tm),
        in_specs=[row, row, pl.BlockSpec((1, 1, d), lambda bi, i: (bi, 0, 0)), par, par],
        out_specs=row, out_shape=jax.ShapeDtypeStruct((b, s, d), F32),
        compiler_params=_params(("parallel", "parallel")), name="final",
    )(x1, y, gate2, ln_g, ln_b)


def _tile(n, want):
    while n % want:
        want //= 2
    return want


def kernel(x, c, w_ada, b_ada, w_in, swa_sinks, group_norm_a, group_norm_b, w_out, ln1_g, ln1_b,
           peer_w_q, peer_sub_keys, peer_u, peer_v, ln2_g, ln2_b):
    b, s, d = x.shape
    n = b * s
    assert d == 8 * LANES and s % BLOCK == 0
    depth = w_ada.shape[0]
    tm = _tile(s, 512)
    for l in range(depth):
        ada = _ada(c, w_ada[l], b_ada[l][None])
        shift1, scale1, gate1, shift2, scale2, gate2 = [a[:, None, :] for a in jnp.split(ada, 6, axis=-1)]

        qa, ka, va, qb, kb, vb = _inproj(x, scale1, shift1, w_in[l].astype(BF16), tm)
        oa = _swa(qa, ka, va, swa_sinks[l])
        ob = _sb(qb, kb, vb, _tile(s, 256))

        gn = jnp.concatenate([group_norm_a[l], group_norm_b[l]]).reshape(-1, 1, HEAD_DIM)
        w_o = w_out[l].astype(BF16).reshape(-1, HEAD_DIM, d)
        x1, h2, q = _outproj(oa, ob, x, gate1, scale2, shift2, gn, w_o, ln1_g[l][None], ln1_b[l][None],
                             peer_w_q[l].astype(BF16), _tile(s, 256))

        sk = peer_sub_keys[l].astype(BF16).reshape(2 * PEER_HEADS, PEER_N_KEYS, PEER_KEY_DIM // 2)
        experts, gates = _topk(q.reshape(n, -1), sk, _tile(n, 128))
        idx = experts.T
        tp = _tile(n, 128)
        wts = _peer_u(idx, h2.reshape(n * SUBLANES, LANES), gates.T, _pack_table(peer_u[l]), tp)
        y = _peer_v(idx, wts, _pack_table(peer_v[l]), tp).reshape(b, s, d)

        x = _final(x1, y, gate2, ln2_g[l][None], ln2_b[l][None], tm)
    return x
```

```python
import functools
import math

import jax
import jax.numpy as jnp
from jax import lax
from jax.experimental import pallas as pl
from jax.experimental.pallas import tpu as pltpu

F32 = jnp.float32
BF16 = jnp.bfloat16
I32 = jnp.int32

HEAD_DIM = 64
SWA_HEADS = 8
SWA_KV_HEADS = 2
SWA_WINDOW = 128
SB_HEADS = 8
BLOCK = 128
SWA_Q = SWA_HEADS * HEAD_DIM
SWA_KV = SWA_KV_HEADS * HEAD_DIM
SB_W = SB_HEADS * HEAD_DIM
PEER_HEADS = 8
PEER_KEY_DIM = 256
PEER_N_KEYS = 128
PEER_TOPK = 16
PICKS = PEER_HEADS * PEER_TOPK
ROW_WORDS = 4
DEPTH = 1
DEEPNORM_ALPHA = (2.0 * DEPTH) ** 0.25
LN_EPS = 1e-5

LANES = 128
SUBLANES = 8
VMEM_LIMIT = 56 * 1024 * 1024


def _params(sem, vmem=None):
    return pltpu.CompilerParams(dimension_semantics=sem, vmem_limit_bytes=vmem)


def _dot_nt(a, b):
    return lax.dot_general(a, b, (((1,), (1,)), ((), ())), preferred_element_type=F32)


def _layer_norm(y, g, b):
    mu = jnp.mean(y, axis=-1, keepdims=True)
    d = y - mu
    var = jnp.mean(d * d, axis=-1, keepdims=True)
    return d * lax.rsqrt(var + LN_EPS) * g + b


def _ada_body(c_ref, w_ref, b_ref, o_ref):
    c = c_ref[...]
    a = c * jax.nn.sigmoid(c)
    o_ref[...] = jnp.dot(a, w_ref[...], preferred_element_type=F32,
                         precision=lax.Precision.HIGHEST) + b_ref[...]


def _ada(c, w, b):
    bsz, d = c.shape
    e = w.shape[1]
    tn = 1024
    return pl.pallas_call(
        _ada_body, grid=(e // tn,),
        in_specs=[pl.BlockSpec((bsz, d), lambda j: (0, 0)),
                  pl.BlockSpec((d, tn), lambda j: (0, j)),
                  pl.BlockSpec((1, tn), lambda j: (0, j))],
        out_specs=pl.BlockSpec((bsz, tn), lambda j: (0, j)),
        out_shape=jax.ShapeDtypeStruct((bsz, e), F32),
        compiler_params=_params(("arbitrary",)), name="ada",
    )(c, w, b)


_IN_SPLITS = (
    (0, SWA_HEADS, 0.125), (SWA_Q, SWA_KV_HEADS, 1.0), (SWA_Q + SWA_KV, SWA_KV_HEADS, 1.0),
    (SWA_Q + 2 * SWA_KV, SB_HEADS, 0.125), (SWA_Q + 2 * SWA_KV + SB_W, SB_HEADS, 1.0),
    (SWA_Q + 2 * SWA_KV + 2 * SB_W, SB_HEADS, 1.0))
IN_COLS = SWA_Q + 2 * SWA_KV + 3 * SB_W
_IN_CHUNK = 256


def _inproj_body(x_ref, sc_ref, sh_ref, w_ref, *out_refs):
    h = (x_ref[0] * (1.0 + sc_ref[0]) + sh_ref[0]).astype(BF16)
    for c in range(IN_COLS // _IN_CHUNK):
        r = jnp.dot(h, w_ref[:, c * _IN_CHUNK:(c + 1) * _IN_CHUNK], preferred_element_type=F32)
        for s in range(_IN_CHUNK // HEAD_DIM):
            col = c * _IN_CHUNK + s * HEAD_DIM
            for (start, heads, scale), o_ref in zip(_IN_SPLITS, out_refs):
                if start <= col < start + heads * HEAD_DIM:
                    piece = r[:, s * HEAD_DIM:(s + 1) * HEAD_DIM]
                    o_ref[0, (col - start) // HEAD_DIM] = (piece * scale).astype(BF16)


def _inproj(x, scale, shift, w_bf16, tm):
    b, s, d = x.shape
    outs = [jax.ShapeDtypeStruct((b, heads, s, HEAD_DIM), BF16) for _, heads, _ in _IN_SPLITS]
    out_specs = [pl.BlockSpec((1, heads, tm, HEAD_DIM), lambda bi, i: (bi, 0, i, 0))
                 for _, heads, _ in _IN_SPLITS]
    return pl.pallas_call(
        _inproj_body, grid=(b, s // tm),
        in_specs=[pl.BlockSpec((1, tm, d), lambda bi, i: (bi, i, 0)),
                  pl.BlockSpec((1, 1, d), lambda bi, i: (bi, 0, 0)),
                  pl.BlockSpec((1, 1, d), lambda bi, i: (bi, 0, 0)),
                  pl.BlockSpec((d, IN_COLS), lambda bi, i: (0, 0))],
        out_specs=out_specs, out_shape=outs,
        compiler_params=_params(("parallel", "parallel"), VMEM_LIMIT), name="inproj",
    )(x, scale, shift, w_bf16)


def _swa_body(sink_ref, q_ref, kp_ref, kc_ref, vp_ref, vc_ref, o_ref):
    j = pl.program_id(1)
    qi = lax.broadcasted_iota(I32, (BLOCK, BLOCK), 0)
    kj = lax.broadcasted_iota(I32, (BLOCK, BLOCK), 1)
    d_cur = (qi - kj).astype(F32)
    d_prev = d_cur + float(BLOCK)
    valid_cur = kj <= qi
    valid_prev = jnp.logical_and(kj > qi, j > 0)
    ratio = SWA_HEADS // SWA_KV_HEADS
    for g in range(SWA_KV_HEADS):
        kp, kc, vp, vc = kp_ref[0, g], kc_ref[0, g], vp_ref[0, g], vc_ref[0, g]
        for r in range(ratio):
            hd = g * ratio + r
            slope = 2.0 ** (-8.0 * (hd + 1) / SWA_HEADS)
            q = q_ref[0, hd]
            sp = jnp.where(valid_prev, _dot_nt(q, kp) - slope * d_prev, -jnp.inf)
            sc = jnp.where(valid_cur, _dot_nt(q, kc) - slope * d_cur, -jnp.inf)
            sink = sink_ref[hd]
            m = jnp.maximum(jnp.maximum(jnp.max(sp, axis=1, keepdims=True),
                                        jnp.max(sc, axis=1, keepdims=True)), sink)
            pp = jnp.exp(sp - m)
            pc = jnp.exp(sc - m)
            den = (jnp.sum(pp, axis=1, keepdims=True) + jnp.sum(pc, axis=1, keepdims=True)
                   + jnp.exp(sink - m))
            o = (jnp.dot((pp / den).astype(BF16), vp, preferred_element_type=F32)
                 + jnp.dot((pc / den).astype(BF16), vc, preferred_element_type=F32))
            o_ref[0, hd] = o


def _swa(qa, ka, va, sinks):
    b, _, s, _ = qa.shape
    nb = s // BLOCK
    cur = lambda bi, j: (bi, 0, j, 0)
    prev = lambda bi, j: (bi, 0, jnp.maximum(j - 1, 0), 0)
    kv_blk = (1, SWA_KV_HEADS, BLOCK, HEAD_DIM)
    return pl.pallas_call(
        _swa_body, grid=(b, nb),
        in_specs=[pl.BlockSpec(memory_space=pltpu.SMEM),
                  pl.BlockSpec((1, SWA_HEADS, BLOCK, HEAD_DIM), cur),
                  pl.BlockSpec(kv_blk, prev), pl.BlockSpec(kv_blk, cur),
                  pl.BlockSpec(kv_blk, prev), pl.BlockSpec(kv_blk, cur)],
        out_specs=pl.BlockSpec((1, SWA_HEADS, BLOCK, HEAD_DIM), cur),
        out_shape=jax.ShapeDtypeStruct((b, SWA_HEADS, s, HEAD_DIM), F32),
        compiler_params=_params(("parallel", "parallel")), name="swa",
    )(sinks, qa, ka, ka, va, va)


SB_HEADS_PER_STEP = 4
LOG2E = 1.4426950408889634


def _sb_body(q_ref, k_ref, v_ref, o_ref, z_scr, carry_scr, *, t):
    i = pl.program_id(2)
    row = lax.broadcasted_iota(I32, (t, t), 0)
    col = lax.broadcasted_iota(I32, (t, t), 1)
    causal = col < row
    tri = (row > col).astype(BF16)
    tri2 = jnp.concatenate([tri, tri], axis=0)
    heads = range(SB_HEADS_PER_STEP)

    def load_scores(j):
        off = pl.multiple_of(j * t, t)
        for hd in heads:
            z_scr[hd] = _dot_nt(q_ref[0, hd], k_ref[0, hd, pl.ds(off, t), :])

    def sweep(j, diagonal):
        off = pl.multiple_of(j * t, t)
        pend = []
        for hd in heads:
            zs = z_scr[hd] * LOG2E
            p = jnp.maximum(zs, 0.0) + jnp.log2(1.0 + jnp.exp2(-jnp.abs(zs)))
            pm = jnp.where(causal, p, 0.0) if diagonal else p
            hi = pm.astype(BF16)
            lo = (pm - hi.astype(F32)).astype(BF16)
            within = jnp.dot(jnp.concatenate([hi, lo], axis=1), tri2, preferred_element_type=F32)
            psum = jnp.broadcast_to(jnp.sum(pm, axis=1, keepdims=True), (t, LANES))
            pend.append((zs - p, within, psum))
        load_scores(jnp.maximum(j - 1, 0))
        for hd in heads:
            base, within, psum = pend[hd]
            e = base - within
            if diagonal:
                a = jnp.where(causal, jnp.exp2(e), 0.0)
                carry_scr[hd] = psum
            else:
                carry = carry_scr[hd]
                a = jnp.exp2(e - jnp.concatenate([carry] * (t // LANES), axis=1))
                carry_scr[hd] = carry + psum
            out = jnp.dot(a.astype(BF16), v_ref[0, hd, pl.ds(off, t), :], preferred_element_type=F32)
            if diagonal:
                o_ref[0, hd] = out
            else:
                o_ref[0, hd] += out

    load_scores(i)
    sweep(i, True)

    def body(jj, _):
        sweep(i - 1 - jj, False)
        return 0

    lax.fori_loop(0, i, body, 0)


def _sb(qb, kb, vb, t):
    b, h, s, _ = qb.shape
    hs = SB_HEADS_PER_STEP
    full = pl.BlockSpec((1, hs, s, HEAD_DIM), lambda bi, hi, i: (bi, hi, 0, 0))
    tile = pl.BlockSpec((1, hs, t, HEAD_DIM), lambda bi, hi, i: (bi, hi, i, 0))
    return pl.pallas_call(
        functools.partial(_sb_body, t=t), grid=(b, h // hs, s // t),
        in_specs=[tile, full, full], out_specs=tile,
        out_shape=jax.ShapeDtypeStruct((b, h, s, HEAD_DIM), F32),
        scratch_shapes=[pltpu.VMEM((hs, t, t), F32), pltpu.VMEM((hs, t, LANES), F32)],
        compiler_params=_params(("parallel", "parallel", "arbitrary"), VMEM_LIMIT), name="sb",
    )(qb, kb, vb)


def _outproj_body(oa_ref, ob_ref, x_ref, g1_ref, sc2_ref, sh2_ref, gn_ref, w_ref, lng_ref, lnb_ref,
                  wq_ref, x1_ref, h2_ref, q_ref):
    def group(o_ref, base, heads):
        ss = None
        for h in range(heads):
            o = o_ref[0, h]
            p = jnp.sum(o * o, axis=1, keepdims=True)
            ss = p if ss is None else ss + p
        inv = lax.rsqrt(ss / float(heads * HEAD_DIM) + LN_EPS)
        mix = None
        for h in range(heads):
            on = (o_ref[0, h] * inv * gn_ref[base + h]).astype(BF16)
            p = jnp.dot(on, w_ref[base + h], preferred_element_type=F32)
            mix = p if mix is None else mix + p
        return mix

    mix = group(oa_ref, 0, SWA_HEADS) + group(ob_ref, SWA_HEADS, SB_HEADS)
    x1 = _layer_norm(DEEPNORM_ALPHA * x_ref[0] + g1_ref[0] * mix, lng_ref[...], lnb_ref[...])
    x1_ref[0] = x1
    h2 = x1 * (1.0 + sc2_ref[0]) + sh2_ref[0]
    h2_ref[0] = h2
    q_ref[0] = jnp.dot(h2.astype(BF16), wq_ref[...], preferred_element_type=F32).astype(BF16)


def _outproj(oa, ob, x, gate1, scale2, shift2, gn, w_out, ln_g, ln_b, wq, tm):
    b, s, d = x.shape
    nq = wq.shape[1]
    nh = SWA_HEADS + SB_HEADS
    row = pl.BlockSpec((1, tm, d), lambda bi, i: (bi, i, 0))
    vec = pl.BlockSpec((1, 1, d), lambda bi, i: (bi, 0, 0))
    par = pl.BlockSpec((1, d), lambda bi, i: (0, 0))
    return pl.pallas_call(
        _outproj_body, grid=(b, s // tm),
        in_specs=[pl.BlockSpec((1, SWA_HEADS, tm, HEAD_DIM), lambda bi, i: (bi, 0, i, 0)),
                  pl.BlockSpec((1, SB_HEADS, tm, HEAD_DIM), lambda bi, i: (bi, 0, i, 0)),
                  row, vec, vec, vec,
                  pl.BlockSpec((nh, 1, HEAD_DIM), lambda bi, i: (0, 0, 0)),
                  pl.BlockSpec((nh, HEAD_DIM, d), lambda bi, i: (0, 0, 0)),
                  par, par,
                  pl.BlockSpec((d, nq), lambda bi, i: (0, 0))],
        out_specs=[row, row, pl.BlockSpec((1, tm, nq), lambda bi, i: (bi, i, 0))],
        out_shape=[jax.ShapeDtypeStruct((b, s, d), F32), jax.ShapeDtypeStruct((b, s, d), F32),
                   jax.ShapeDtypeStruct((b, s, nq), BF16)],
        compiler_params=_params(("parallel", "parallel"), VMEM_LIMIT), name="outproj",
    )(oa, ob, x, gate1, scale2, shift2, gn, w_out, ln_g, ln_b, wq)


def _top_rows(s, ids, k):
    beyond = 1.0e6
    vals, picks = [], []
    for _ in range(k):
        m = jnp.max(s, axis=0, keepdims=True)
        first = jnp.min(jnp.where(s == m, ids, beyond), axis=0, keepdims=True)
        vals.append(m)
        picks.append(first)
        s = jnp.where(ids == first, -jnp.inf, s)
    return jnp.concatenate(vals, axis=0), jnp.concatenate(picks, axis=0)


def _take_rows(table, sel):
    out = jnp.zeros_like(sel)
    for r in range(table.shape[0]):
        out = jnp.where(sel == float(r), table[r:r + 1], out)
    return out


_PAIR_ROWS = PEER_TOPK + 7 * SUBLANES + SUBLANES


def _pair_ids(t):
    r = lax.broadcasted_iota(I32, (_PAIR_ROWS, t), 0)
    mid = r - PEER_TOPK
    mid_id = ((mid >> 3) + 1) * PEER_TOPK + (mid & 7)
    tail_id = (r - (PEER_TOPK + 7 * SUBLANES) + SUBLANES) * PEER_TOPK
    ids = jnp.where(r < PEER_TOPK, r, jnp.where(r < PEER_TOPK + 7 * SUBLANES, mid_id, tail_id))
    return ids.astype(F32)


def _topk_body(q_ref, sk_ref, e_ref, g_ref):
    t = q_ref.shape[0]
    key_ids = lax.broadcasted_iota(I32, (PEER_N_KEYS, t), 0).astype(F32)
    pair_ids = _pair_ids(t)
    half = PEER_KEY_DIM // 2
    for h in range(PEER_HEADS):
        tops = []
        for p in range(2):
            qhp = q_ref[:, (2 * h + p) * half:(2 * h + p + 1) * half]
            sc = _dot_nt(sk_ref[2 * h + p], qhp)
            tops.append(_top_rows(sc, key_ids, PEER_TOPK))
        (s1, i1), (s2, i2) = tops
        cand = jnp.concatenate([s1[0:1] + s2]
                               + [s1[a:a + 1] + s2[0:SUBLANES] for a in range(1, SUBLANES)]
                               + [s1[SUBLANES:] + s2[0:1]], axis=0)
        best, flat = _top_rows(cand, pair_ids, PEER_TOPK)
        a_sel = jnp.floor(flat * (1.0 / PEER_TOPK))
        b_sel = flat - a_sel * PEER_TOPK
        expert = _take_rows(i1, a_sel) * PEER_N_KEYS + _take_rows(i2, b_sel)
        ex = jnp.exp(best - best[0:1])
        e_ref[h * PEER_TOPK:(h + 1) * PEER_TOPK, :] = expert.astype(I32) * ROW_WORDS
        g_ref[h * PEER_TOPK:(h + 1) * PEER_TOPK, :] = ex / jnp.sum(ex, axis=0, keepdims=True)


def _topk(q, sub_keys_bf16, t):
    n, nq = q.shape
    return pl.pallas_call(
        _topk_body, grid=(n // t,),
        in_specs=[pl.BlockSpec((t, nq), lambda i: (i, 0)),
                  pl.BlockSpec(sub_keys_bf16.shape, lambda i: (0, 0, 0))],
        out_specs=[pl.BlockSpec((PICKS, t), lambda i: (0, i)), pl.BlockSpec((PICKS, t), lambda i: (0, i))],
        out_shape=[jax.ShapeDtypeStruct((PICKS, n), I32), jax.ShapeDtypeStruct((PICKS, n), F32)],
        compiler_params=_params(("parallel",), VMEM_LIMIT), name="topk",
    )(q, sub_keys_bf16)


def _pack_table(tbl):
    e, d = tbl.shape
    tb = tbl.astype(BF16).reshape(e, d // (2 * LANES), 2, LANES).transpose(0, 1, 3, 2)
    return lax.bitcast_convert_type(tb, jnp.uint32).reshape(e * (d // (2 * LANES)), LANES)


def _gelu_tanh(x):
    return 0.5 * x * (1.0 + jnp.tanh(math.sqrt(2.0 / math.pi) * (x + 0.044715 * (x * x * x))))


SLAB_ROWS = PICKS * ROW_WORDS
SLAB_COLS = PICKS * SUBLANES
TOKENS_PER_STEP = 4


def _expand_matrix():
    k = lax.broadcasted_iota(I32, (PICKS, SLAB_COLS), 0)
    c = lax.broadcasted_iota(I32, (PICKS, SLAB_COLS), 1)
    return (c // SUBLANES == k).astype(BF16)


def _gather_slab(idx_ref, tbl_ref, slab_ref, tok):
    rows = idx_ref.at[tok]
    for k in range(PICKS):
        src = pl.multiple_of(rows[k], ROW_WORDS)
        slab_ref[k * ROW_WORDS:(k + 1) * ROW_WORDS, :] = tbl_ref[pl.ds(src, ROW_WORDS), :]


def _split_rows(a):
    hi = a.astype(BF16).astype(F32)
    return jnp.concatenate([hi, a - hi], axis=0).astype(BF16)


def _chunk_diag():
    sub = lax.broadcasted_iota(I32, (SUBLANES, SLAB_COLS), 0)
    col = lax.broadcasted_iota(I32, (SUBLANES, SLAB_COLS), 1)
    return (col & (SUBLANES - 1)) == sub


def _pipelined_tokens(t, idx_ref, tbl_ref, slabs, compute_group):
    sets = (slabs[:TOKENS_PER_STEP], slabs[TOKENS_PER_STEP:])

    def gather_group(g, slab_set):
        for s, slab in enumerate(slab_set):
            _gather_slab(idx_ref, tbl_ref, slab, jnp.minimum(g * TOKENS_PER_STEP + s, t - 1))

    gather_group(0, sets[0])

    def step(p, _):
        gather_group(2 * p + 1, sets[1])
        compute_group(pl.multiple_of(2 * p * TOKENS_PER_STEP, TOKENS_PER_STEP), sets[0])
        gather_group(2 * p + 2, sets[0])
        compute_group(pl.multiple_of((2 * p + 1) * TOKENS_PER_STEP, TOKENS_PER_STEP), sets[1])
        return 0

    lax.fori_loop(0, t // (2 * TOKENS_PER_STEP), step, 0)


def _peer_u_body(idx_ref, x_ref, g_ref, tbl_ref, grp_ref, o_ref, h_scr, *slabs, t):
    diag = _chunk_diag()

    def compute_group(tok0, slab_set):
        parts = []
        for s, slab in enumerate(slab_set):
            xt = x_ref[pl.ds(pl.multiple_of((tok0 + s) * SUBLANES, SUBLANES), SUBLANES), :]
            r = _dot_nt(_split_rows(xt), pltpu.bitcast(slab[...], BF16))
            parts.append(_split_rows(jnp.where(diag, r[:SUBLANES] + r[SUBLANES:], 0.0)))
        hh = jnp.dot(jnp.concatenate(parts, axis=0), grp_ref[...], preferred_element_type=F32)
        rows = [jnp.sum(hh[2 * SUBLANES * s:2 * SUBLANES * (s + 1)], axis=0, keepdims=True)
                for s in range(len(slab_set))]
        h_scr[pl.ds(tok0, len(slab_set)), :] = jnp.concatenate(rows, axis=0)

    _pipelined_tokens(t, idx_ref, tbl_ref, slabs, compute_group)
    o_ref[...] = g_ref[...] * _gelu_tanh(h_scr[...])


def _peer_u(idx, x8, gates, tbl, t):
    n = idx.shape[0]
    grp = _expand_matrix().T
    return pl.pallas_call(
        functools.partial(_peer_u_body, t=t), grid=(n // t,),
        in_specs=[pl.BlockSpec((t, PICKS), lambda i: (i, 0), memory_space=pltpu.SMEM),
                  pl.BlockSpec((t * SUBLANES, LANES), lambda i: (i, 0)),
                  pl.BlockSpec((t, PICKS), lambda i: (i, 0)),
                  pl.BlockSpec(tbl.shape, lambda i: (0, 0), pipeline_mode=pl.Buffered(1)),
                  pl.BlockSpec(grp.shape, lambda i: (0, 0), pipeline_mode=pl.Buffered(1))],
        out_specs=pl.BlockSpec((t, PICKS), lambda i: (i, 0)),
        out_shape=jax.ShapeDtypeStruct((n, PICKS), F32),
        scratch_shapes=[pltpu.VMEM((t, PICKS), F32)]
        + [pltpu.VMEM((SLAB_ROWS, LANES), jnp.uint32)] * (2 * TOKENS_PER_STEP),
        compiler_params=_params(("arbitrary",), VMEM_LIMIT), name="peer_u",
    )(idx, x8, gates, tbl, grp)


def _peer_v_body(idx_ref, w_ref, tbl_ref, exp_ref, o_ref, rep_hi, rep_lo, *slabs, t):
    diag = _chunk_diag()
    w = w_ref[...]
    w_hi = w.astype(BF16)
    w_lo = (w - w_hi.astype(F32)).astype(BF16)
    rep_hi[...] = jnp.dot(w_hi, exp_ref[...], preferred_element_type=F32)
    rep_lo[...] = jnp.dot(w_lo, exp_ref[...], preferred_element_type=F32)

    def compute_group(tok0, slab_set):
        for s, slab in enumerate(slab_set):
            tok = tok0 + s
            lhs = jnp.concatenate([jnp.where(diag, rep_hi[pl.ds(tok, 1), :], 0.0),
                                   jnp.where(diag, rep_lo[pl.ds(tok, 1), :], 0.0)], axis=0).astype(BF16)
            out = jnp.dot(lhs, pltpu.bitcast(slab[...], BF16), preferred_element_type=F32)
            o_ref[pl.ds(pl.multiple_of(tok * SUBLANES, SUBLANES), SUBLANES), :] = (
                out[:SUBLANES] + out[SUBLANES:])

    _pipelined_tokens(t, idx_ref, tbl_ref, slabs, compute_group)


def _peer_v(idx, wts, tbl, t):
    n = idx.shape[0]
    exp = _expand_matrix()
    return pl.pallas_call(
        functools.partial(_peer_v_body, t=t), grid=(n // t,),
        in_specs=[pl.BlockSpec((t, PICKS), lambda i: (i, 0), memory_space=pltpu.SMEM),
                  pl.BlockSpec((t, PICKS), lambda i: (i, 0)),
                  pl.BlockSpec(tbl.shape, lambda i: (0, 0), pipeline_mode=pl.Buffered(1)),
                  pl.BlockSpec(exp.shape, lambda i: (0, 0), pipeline_mode=pl.Buffered(1))],
        out_specs=pl.BlockSpec((t * SUBLANES, LANES), lambda i: (i, 0)),
        out_shape=jax.ShapeDtypeStruct((n * SUBLANES, LANES), F32),
        scratch_shapes=[pltpu.VMEM((t, SLAB_COLS), F32)] * 2
        + [pltpu.VMEM((SLAB_ROWS, LANES), jnp.uint32)] * (2 * TOKENS_PER_STEP),
        compiler_params=_params(("arbitrary",), VMEM_LIMIT), name="peer_v",
    )(idx, wts, tbl, exp)


def _final_body(x_ref, y_ref, g2_ref, lng_ref, lnb_ref, o_ref):
    o_ref[0] = _layer_norm(DEEPNORM_ALPHA * x_ref[0] + g2_ref[0] * y_ref[0], lng_ref[...], lnb_ref[...])


def _final(x1, y, gate2, ln_g, ln_b, tm):
    b, s, d = x1.shape
    row = pl.BlockSpec((1, tm, d), lambda bi, i: (bi, i, 0))
    par = pl.BlockSpec((1, d), lambda bi, i: (0, 0))
    return pl.pallas_call(
        _final_body, grid=(b, s // tm),
        in_specs=[row, row, pl.BlockSpec((1, 1, d), lambda bi, i: (bi, 0, 0)), par, par],
        out_specs=row, out_shape=jax.ShapeDtypeStruct((b, s, d), F32),
        compiler_params=_params(("parallel", "parallel")), name="final",
    )(x1, y, gate2, ln_g, ln_b)


def _tile(n, want):
    while n % want:
        want //= 2
    return want


def kernel(x, c, w_ada, b_ada, w_in, swa_sinks, group_norm_a, group_norm_b, w_out, ln1_g, ln1_b,
           peer_w_q, peer_sub_keys, peer_u, peer_v, ln2_g, ln2_b):
    b, s, d = x.shape
    n = b * s
    assert d == 8 * LANES and s % BLOCK == 0
    depth = w_ada.shape[0]
    tm = _tile(s, 512)
    for l in range(depth):
        ada = _ada(c, w_ada[l], b_ada[l][None])
        shift1, scale1, gate1, shift2, scale2, gate2 = [a[:, None, :] for a in jnp.split(ada, 6, axis=-1)]

        qa, ka, va, qb, kb, vb = _inproj(x, scale1, shift1, w_in[l].astype(BF16), tm)
        oa = _swa(qa, ka, va, swa_sinks[l])
        ob = _sb(qb, kb, vb, _tile(s, 256))

        gn = jnp.concatenate([group_norm_a[l], group_norm_b[l]]).reshape(-1, 1, HEAD_DIM)
        w_o = w_out[l].astype(BF16).reshape(-1, HEAD_DIM, d)
        x1, h2, q = _outproj(oa, ob, x, gate1, scale2, shift2, gn, w_o, ln1_g[l][None], ln1_b[l][None],
                             peer_w_q[l].astype(BF16), _tile(s, 256))

        sk = peer_sub_keys[l].astype(BF16).reshape(2 * PEER_HEADS, PEER_N_KEYS, PEER_KEY_DIM // 2)
        experts, gates = _topk(q.reshape(n, -1), sk, _tile(n, 128))
        idx = experts.T
        tp = _tile(n, 128)
        wts = _peer_u(idx, h2.reshape(n * SUBLANES, LANES), gates.T, _pack_table(peer_u[l]), tp)
        y = _peer_v(idx, wts, _pack_table(peer_v[l]), tp).reshape(b, s, d)

        x = _final(x1, y, gate2, ln2_g[l][None], ln2_b[l][None], tm)
    return x
```

```python
import functools
import math

import jax
import jax.numpy as jnp
from jax import lax
from jax.experimental import pallas as pl
from jax.experimental.pallas import tpu as pltpu

F32 = jnp.float32
BF16 = jnp.bfloat16
I32 = jnp.int32

HEAD_DIM = 64
SWA_HEADS = 8
SWA_KV_HEADS = 2
SWA_WINDOW = 128
SB_HEADS = 8
BLOCK = 128
SWA_Q = SWA_HEADS * HEAD_DIM
SWA_KV = SWA_KV_HEADS * HEAD_DIM
SB_W = SB_HEADS * HEAD_DIM
PEER_HEADS = 8
PEER_KEY_DIM = 256
PEER_N_KEYS = 128
PEER_TOPK = 16
PICKS = PEER_HEADS * PEER_TOPK
ROW_WORDS = 4
DEPTH = 1
DEEPNORM_ALPHA = (2.0 * DEPTH) ** 0.25
LN_EPS = 1e-5

LANES = 128
SUBLANES = 8
VMEM_LIMIT = 56 * 1024 * 1024


def _params(sem, vmem=None):
    return pltpu.CompilerParams(dimension_semantics=sem, vmem_limit_bytes=vmem)


def _dot_nt(a, b):
    return lax.dot_general(a, b, (((1,), (1,)), ((), ())), preferred_element_type=F32)


def _layer_norm(y, g, b):
    mu = jnp.mean(y, axis=-1, keepdims=True)
    d = y - mu
    var = jnp.mean(d * d, axis=-1, keepdims=True)
    return d * lax.rsqrt(var + LN_EPS) * g + b


def _ada_body(c_ref, w_ref, b_ref, o_ref):
    c = c_ref[...]
    a = c * jax.nn.sigmoid(c)
    o_ref[...] = jnp.dot(a, w_ref[...], preferred_element_type=F32,
                         precision=lax.Precision.HIGHEST) + b_ref[...]


def _ada(c, w, b):
    bsz, d = c.shape
    e = w.shape[1]
    tn = 1024
    return pl.pallas_call(
        _ada_body, grid=(e // tn,),
        in_specs=[pl.BlockSpec((bsz, d), lambda j: (0, 0)),
                  pl.BlockSpec((d, tn), lambda j: (0, j)),
                  pl.BlockSpec((1, tn), lambda j: (0, j))],
        out_specs=pl.BlockSpec((bsz, tn), lambda j: (0, j)),
        out_shape=jax.ShapeDtypeStruct((bsz, e), F32),
        compiler_params=_params(("arbitrary",)), name="ada",
    )(c, w, b)


_IN_SPLITS = (
    (0, SWA_HEADS, 0.125), (SWA_Q, SWA_KV_HEADS, 1.0), (SWA_Q + SWA_KV, SWA_KV_HEADS, 1.0),
    (SWA_Q + 2 * SWA_KV, SB_HEADS, 0.125), (SWA_Q + 2 * SWA_KV + SB_W, SB_HEADS, 1.0),
    (SWA_Q + 2 * SWA_KV + 2 * SB_W, SB_HEADS, 1.0))
IN_COLS = SWA_Q + 2 * SWA_KV + 3 * SB_W
_IN_CHUNK = 256


def _inproj_body(x_ref, sc_ref, sh_ref, w_ref, *out_refs):
    h = (x_ref[0] * (1.0 + sc_ref[0]) + sh_ref[0]).astype(BF16)
    for c in range(IN_COLS // _IN_CHUNK):
        r = jnp.dot(h, w_ref[:, c * _IN_CHUNK:(c + 1) * _IN_CHUNK], preferred_element_type=F32)
        for s in range(_IN_CHUNK // HEAD_DIM):
            col = c * _IN_CHUNK + s * HEAD_DIM
            for (start, heads, scale), o_ref in zip(_IN_SPLITS, out_refs):
                if start <= col < start + heads * HEAD_DIM:
                    piece = r[:, s * HEAD_DIM:(s + 1) * HEAD_DIM]
                    o_ref[0, (col - start) // HEAD_DIM] = (piece * scale).astype(BF16)


def _inproj(x, scale, shift, w_bf16, tm):
    b, s, d = x.shape
    outs = [jax.ShapeDtypeStruct((b, heads, s, HEAD_DIM), BF16) for _, heads, _ in _IN_SPLITS]
    out_specs = [pl.BlockSpec((1, heads, tm, HEAD_DIM), lambda bi, i: (bi, 0, i, 0))
                 for _, heads, _ in _IN_SPLITS]
    return pl.pallas_call(
        _inproj_body, grid=(b, s // tm),
        in_specs=[pl.BlockSpec((1, tm, d), lambda bi, i: (bi, i, 0)),
                  pl.BlockSpec((1, 1, d), lambda bi, i: (bi, 0, 0)),
                  pl.BlockSpec((1, 1, d), lambda bi, i: (bi, 0, 0)),
                  pl.BlockSpec((d, IN_COLS), lambda bi, i: (0, 0))],
        out_specs=out_specs, out_shape=outs,
        compiler_params=_params(("parallel", "parallel"), VMEM_LIMIT), name="inproj",
    )(x, scale, shift, w_bf16)


def _swa_body(sink_ref, q_ref, kp_ref, kc_ref, vp_ref, vc_ref, o_ref):
    j = pl.program_id(1)
    ratio = SWA_HEADS // SWA_KV_HEADS
    rows = ratio * BLOCK
    qi = lax.broadcasted_iota(I32, (rows, BLOCK), 0) & (BLOCK - 1)
    kj = lax.broadcasted_iota(I32, (rows, BLOCK), 1)
    d_cur = (qi - kj).astype(F32)
    d_prev = d_cur + float(BLOCK)
    valid_cur = kj <= qi
    valid_prev = jnp.logical_and(kj > qi, j > 0)
    groups = range(SWA_KV_HEADS)

    def column(values):
        return jnp.concatenate([jnp.full((BLOCK, 1), v, F32) for v in values], axis=0)

    scores = []
    for g in groups:
        q = jnp.concatenate([q_ref[0, g * ratio + r] for r in range(ratio)], axis=0)
        scores.append((_dot_nt(q, kp_ref[0, g]), _dot_nt(q, kc_ref[0, g])))
    probs = []
    for g in groups:
        heads = [g * ratio + r for r in range(ratio)]
        slope = column([2.0 ** (-8.0 * (hd + 1) / SWA_HEADS) for hd in heads])
        sink = column([sink_ref[hd] for hd in heads])
        sp = jnp.where(valid_prev, scores[g][0] - slope * d_prev, -jnp.inf)
        sc = jnp.where(valid_cur, scores[g][1] - slope * d_cur, -jnp.inf)
        m = jnp.maximum(jnp.maximum(jnp.max(sp, axis=1, keepdims=True),
                                    jnp.max(sc, axis=1, keepdims=True)), sink)
        pp = jnp.exp(sp - m)
        pc = jnp.exp(sc - m)
        den = (jnp.sum(pp, axis=1, keepdims=True) + jnp.sum(pc, axis=1, keepdims=True)
               + jnp.exp(sink - m))
        probs.append(((pp / den).astype(BF16), (pc / den).astype(BF16)))
    for g in groups:
        o = (jnp.dot(probs[g][0], vp_ref[0, g], preferred_element_type=F32)
             + jnp.dot(probs[g][1], vc_ref[0, g], preferred_element_type=F32))
        for r in range(ratio):
            o_ref[0, g * ratio + r] = o[r * BLOCK:(r + 1) * BLOCK]


def _swa(qa, ka, va, sinks):
    b, _, s, _ = qa.shape
    nb = s // BLOCK
    cur = lambda bi, j: (bi, 0, j, 0)
    prev = lambda bi, j: (bi, 0, jnp.maximum(j - 1, 0), 0)
    kv_blk = (1, SWA_KV_HEADS, BLOCK, HEAD_DIM)
    return pl.pallas_call(
        _swa_body, grid=(b, nb),
        in_specs=[pl.BlockSpec(memory_space=pltpu.SMEM),
                  pl.BlockSpec((1, SWA_HEADS, BLOCK, HEAD_DIM), cur),
                  pl.BlockSpec(kv_blk, prev), pl.BlockSpec(kv_blk, cur),
                  pl.BlockSpec(kv_blk, prev), pl.BlockSpec(kv_blk, cur)],
        out_specs=pl.BlockSpec((1, SWA_HEADS, BLOCK, HEAD_DIM), cur),
        out_shape=jax.ShapeDtypeStruct((b, SWA_HEADS, s, HEAD_DIM), F32),
        compiler_params=_params(("parallel", "parallel")), name="swa",
    )(sinks, qa, ka, ka, va, va)


SB_HEADS_PER_STEP = 4
LOG2E = 1.4426950408889634


def _sb_body(q_ref, k_ref, v_ref, o_ref, z_scr, carry_scr, *, t):
    i = pl.program_id(2)
    row = lax.broadcasted_iota(I32, (t, t), 0)
    col = lax.broadcasted_iota(I32, (t, t), 1)
    causal = col < row
    tri = (row > col).astype(BF16)
    tri2 = jnp.concatenate([tri, tri], axis=0)
    heads = range(SB_HEADS_PER_STEP)

    def load_scores(j):
        off = pl.multiple_of(j * t, t)
        for hd in heads:
            z_scr[hd] = _dot_nt(q_ref[0, hd], k_ref[0, hd, pl.ds(off, t), :])

    def sweep(j, diagonal):
        off = pl.multiple_of(j * t, t)
        pend = []
        for hd in heads:
            zs = z_scr[hd] * LOG2E
            p = jnp.maximum(zs, 0.0) + jnp.log2(1.0 + jnp.exp2(-jnp.abs(zs)))
            pm = jnp.where(causal, p, 0.0) if diagonal else p
            hi = pm.astype(BF16)
            lo = (pm - hi.astype(F32)).astype(BF16)
            within = jnp.dot(jnp.concatenate([hi, lo], axis=1), tri2, preferred_element_type=F32)
            psum = jnp.broadcast_to(jnp.sum(pm, axis=1, keepdims=True), (t, LANES))
            pend.append((zs - p, within, psum))
        load_scores(jnp.maximum(j - 1, 0))
        for hd in heads:
            base, within, psum = pend[hd]
            e = base - within
            if diagonal:
                a = jnp.where(causal, jnp.exp2(e), 0.0)
                carry_scr[hd] = psum
            else:
                carry = carry_scr[hd]
                a = jnp.exp2(e - jnp.concatenate([carry] * (t // LANES), axis=1))
                carry_scr[hd] = carry + psum
            out = jnp.dot(a.astype(BF16), v_ref[0, hd, pl.ds(off, t), :], preferred_element_type=F32)
            if diagonal:
                o_ref[0, hd] = out
            else:
                o_ref[0, hd] += out

    load_scores(i)
    sweep(i, True)

    def body(jj, _):
        sweep(i - 1 - jj, False)
        return 0

    lax.fori_loop(0, i, body, 0)


def _sb(qb, kb, vb, t):
    b, h, s, _ = qb.shape
    hs = SB_HEADS_PER_STEP
    full = pl.BlockSpec((1, hs, s, HEAD_DIM), lambda bi, hi, i: (bi, hi, 0, 0))
    tile = pl.BlockSpec((1, hs, t, HEAD_DIM), lambda bi, hi, i: (bi, hi, i, 0))
    return pl.pallas_call(
        functools.partial(_sb_body, t=t), grid=(b, h // hs, s // t),
        in_specs=[tile, full, full], out_specs=tile,
        out_shape=jax.ShapeDtypeStruct((b, h, s, HEAD_DIM), F32),
        scratch_shapes=[pltpu.VMEM((hs, t, t), F32), pltpu.VMEM((hs, t, LANES), F32)],
        compiler_params=_params(("parallel", "parallel", "arbitrary"), VMEM_LIMIT), name="sb",
    )(qb, kb, vb)


def _outproj_body(oa_ref, ob_ref, x_ref, g1_ref, sc2_ref, sh2_ref, gn_ref, w_ref, lng_ref, lnb_ref,
                  wq_ref, x1_ref, h2_ref, q_ref):
    def group(o_ref, base, heads):
        ss = None
        for h in range(heads):
            o = o_ref[0, h]
            p = jnp.sum(o * o, axis=1, keepdims=True)
            ss = p if ss is None else ss + p
        inv = lax.rsqrt(ss / float(heads * HEAD_DIM) + LN_EPS)
        mix = None
        for h in range(heads):
            on = (o_ref[0, h] * inv * gn_ref[base + h]).astype(BF16)
            p = jnp.dot(on, w_ref[base + h], preferred_element_type=F32)
            mix = p if mix is None else mix + p
        return mix

    mix = group(oa_ref, 0, SWA_HEADS) + group(ob_ref, SWA_HEADS, SB_HEADS)
    x1 = _layer_norm(DEEPNORM_ALPHA * x_ref[0] + g1_ref[0] * mix, lng_ref[...], lnb_ref[...])
    x1_ref[0] = x1
    h2 = x1 * (1.0 + sc2_ref[0]) + sh2_ref[0]
    h2_ref[0] = h2
    q_ref[0] = jnp.dot(h2.astype(BF16), wq_ref[...], preferred_element_type=F32).astype(BF16)


def _outproj(oa, ob, x, gate1, scale2, shift2, gn, w_out, ln_g, ln_b, wq, tm):
    b, s, d = x.shape
    nq = wq.shape[1]
    nh = SWA_HEADS + SB_HEADS
    row = pl.BlockSpec((1, tm, d), lambda bi, i: (bi, i, 0))
    vec = pl.BlockSpec((1, 1, d), lambda bi, i: (bi, 0, 0))
    par = pl.BlockSpec((1, d), lambda bi, i: (0, 0))
    return pl.pallas_call(
        _outproj_body, grid=(b, s // tm),
        in_specs=[pl.BlockSpec((1, SWA_HEADS, tm, HEAD_DIM), lambda bi, i: (bi, 0, i, 0)),
                  pl.BlockSpec((1, SB_HEADS, tm, HEAD_DIM), lambda bi, i: (bi, 0, i, 0)),
                  row, vec, vec, vec,
                  pl.BlockSpec((nh, 1, HEAD_DIM), lambda bi, i: (0, 0, 0)),
                  pl.BlockSpec((nh, HEAD_DIM, d), lambda bi, i: (0, 0, 0)),
                  par, par,
                  pl.BlockSpec((d, nq), lambda bi, i: (0, 0))],
        out_specs=[row, row, pl.BlockSpec((1, tm, nq), lambda bi, i: (bi, i, 0))],
        out_shape=[jax.ShapeDtypeStruct((b, s, d), F32), jax.ShapeDtypeStruct((b, s, d), F32),
                   jax.ShapeDtypeStruct((b, s, nq), BF16)],
        compiler_params=_params(("parallel", "parallel"), VMEM_LIMIT), name="outproj",
    )(oa, ob, x, gate1, scale2, shift2, gn, w_out, ln_g, ln_b, wq)


def _top_rows(s, ids, k):
    beyond = 1.0e6
    vals, picks = [], []
    for _ in range(k):
        m = jnp.max(s, axis=0, keepdims=True)
        first = jnp.min(jnp.where(s == m, ids, beyond), axis=0, keepdims=True)
        vals.append(m)
        picks.append(first)
        s = jnp.where(ids == first, -jnp.inf, s)
    return jnp.concatenate(vals, axis=0), jnp.concatenate(picks, axis=0)


def _take_rows(table, sel):
    out = jnp.zeros_like(sel)
    for r in range(table.shape[0]):
        out = jnp.where(sel == float(r), table[r:r + 1], out)
    return out


_PAIR_ROWS = PEER_TOPK + 7 * SUBLANES + SUBLANES


def _pair_ids(t):
    r = lax.broadcasted_iota(I32, (_PAIR_ROWS, t), 0)
    mid = r - PEER_TOPK
    mid_id = ((mid >> 3) + 1) * PEER_TOPK + (mid & 7)
    tail_id = (r - (PEER_TOPK + 7 * SUBLANES) + SUBLANES) * PEER_TOPK
    ids = jnp.where(r < PEER_TOPK, r, jnp.where(r < PEER_TOPK + 7 * SUBLANES, mid_id, tail_id))
    return ids.astype(F32)


def _topk_body(q_ref, sk_ref, e_ref, g_ref):
    t = q_ref.shape[0]
    key_ids = lax.broadcasted_iota(I32, (PEER_N_KEYS, t), 0).astype(F32)
    pair_ids = _pair_ids(t)
    half = PEER_KEY_DIM // 2
    rows_out, gates_out = [], []
    for h in range(PEER_HEADS):
        tops = []
        for p in range(2):
            qhp = q_ref[:, (2 * h + p) * half:(2 * h + p + 1) * half]
            sc = _dot_nt(sk_ref[2 * h + p], qhp)
            tops.append(_top_rows(sc, key_ids, PEER_TOPK))
        (s1, i1), (s2, i2) = tops
        cand = jnp.concatenate([s1[0:1] + s2]
                               + [s1[a:a + 1] + s2[0:SUBLANES] for a in range(1, SUBLANES)]
                               + [s1[SUBLANES:] + s2[0:1]], axis=0)
        best, flat = _top_rows(cand, pair_ids, PEER_TOPK)
        a_sel = jnp.floor(flat * (1.0 / PEER_TOPK))
        b_sel = flat - a_sel * PEER_TOPK
        expert = _take_rows(i1, a_sel) * PEER_N_KEYS + _take_rows(i2, b_sel)
        ex = jnp.exp(best - best[0:1])
        rows_out.append(expert * float(ROW_WORDS))
        gates_out.append(ex / jnp.sum(ex, axis=0, keepdims=True))
    e_ref[...] = jnp.concatenate(rows_out, axis=0).T.astype(I32)
    g_ref[...] = jnp.concatenate(gates_out, axis=0).T


def _topk(q, sub_keys_bf16, t):
    n, nq = q.shape
    return pl.pallas_call(
        _topk_body, grid=(n // t,),
        in_specs=[pl.BlockSpec((t, nq), lambda i: (i, 0)),
                  pl.BlockSpec(sub_keys_bf16.shape, lambda i: (0, 0, 0))],
        out_specs=[pl.BlockSpec((t, PICKS), lambda i: (i, 0)), pl.BlockSpec((t, PICKS), lambda i: (i, 0))],
        out_shape=[jax.ShapeDtypeStruct((n, PICKS), I32), jax.ShapeDtypeStruct((n, PICKS), F32)],
        compiler_params=_params(("parallel",), VMEM_LIMIT), name="topk",
    )(q, sub_keys_bf16)


def _pack_table(tbl):
    e, d = tbl.shape
    tb = tbl.astype(BF16).reshape(e, d // (2 * LANES), 2, LANES).transpose(0, 1, 3, 2)
    return lax.bitcast_convert_type(tb, jnp.uint32).reshape(e * (d // (2 * LANES)), LANES)


def _gelu_tanh(x):
    return 0.5 * x * (1.0 + jnp.tanh(math.sqrt(2.0 / math.pi) * (x + 0.044715 * (x * x * x))))


SLAB_ROWS = PICKS * ROW_WORDS
SLAB_COLS = PICKS * SUBLANES
TOKENS_PER_STEP = 4


def _expand_matrix():
    k = lax.broadcasted_iota(I32, (PICKS, SLAB_COLS), 0)
    c = lax.broadcasted_iota(I32, (PICKS, SLAB_COLS), 1)
    return (c // SUBLANES == k).astype(BF16)


def _gather_slab(idx_ref, tbl_ref, slab_ref, tok):
    rows = idx_ref.at[tok]
    for k in range(PICKS):
        src = pl.multiple_of(rows[k], ROW_WORDS)
        slab_ref[k * ROW_WORDS:(k + 1) * ROW_WORDS, :] = tbl_ref[pl.ds(src, ROW_WORDS), :]


def _split_rows(a):
    hi = a.astype(BF16).astype(F32)
    return jnp.concatenate([hi, a - hi], axis=0).astype(BF16)


def _chunk_diag():
    sub = lax.broadcasted_iota(I32, (SUBLANES, SLAB_COLS), 0)
    col = lax.broadcasted_iota(I32, (SUBLANES, SLAB_COLS), 1)
    return (col & (SUBLANES - 1)) == sub


def _pipelined_tokens(t, idx_ref, tbl_ref, slabs, compute_group):
    sets = (slabs[:TOKENS_PER_STEP], slabs[TOKENS_PER_STEP:])

    def gather_group(g, slab_set):
        for s, slab in enumerate(slab_set):
            _gather_slab(idx_ref, tbl_ref, slab, jnp.minimum(g * TOKENS_PER_STEP + s, t - 1))

    gather_group(0, sets[0])

    def step(p, _):
        gather_group(2 * p + 1, sets[1])
        compute_group(pl.multiple_of(2 * p * TOKENS_PER_STEP, TOKENS_PER_STEP), sets[0])
        gather_group(2 * p + 2, sets[0])
        compute_group(pl.multiple_of((2 * p + 1) * TOKENS_PER_STEP, TOKENS_PER_STEP), sets[1])
        return 0

    lax.fori_loop(0, t // (2 * TOKENS_PER_STEP), step, 0)


def _peer_u_body(idx_ref, x_ref, g_ref, tbl_ref, grp_ref, o_ref, h_scr, x_scr, *slabs, t):
    diag = _chunk_diag()
    for j in range(SUBLANES):
        x_scr[pl.ds(j, t, stride=SUBLANES), :] = x_ref[:, j * LANES:(j + 1) * LANES]

    def compute_group(tok0, slab_set):
        parts = []
        for s, slab in enumerate(slab_set):
            xt = x_scr[pl.ds(pl.multiple_of((tok0 + s) * SUBLANES, SUBLANES), SUBLANES), :]
            r = _dot_nt(_split_rows(xt), pltpu.bitcast(slab[...], BF16))
            parts.append(_split_rows(jnp.where(diag, r[:SUBLANES] + r[SUBLANES:], 0.0)))
        hh = jnp.dot(jnp.concatenate(parts, axis=0), grp_ref[...], preferred_element_type=F32)
        rows = [jnp.sum(hh[2 * SUBLANES * s:2 * SUBLANES * (s + 1)], axis=0, keepdims=True)
                for s in range(len(slab_set))]
        h_scr[pl.ds(tok0, len(slab_set)), :] = jnp.concatenate(rows, axis=0)

    _pipelined_tokens(t, idx_ref, tbl_ref, slabs, compute_group)
    o_ref[...] = g_ref[...] * _gelu_tanh(h_scr[...])


def _peer_u(idx, x, gates, tbl, t):
    n, d = x.shape
    grp = _expand_matrix().T
    return pl.pallas_call(
        functools.partial(_peer_u_body, t=t), grid=(n // t,),
        in_specs=[pl.BlockSpec((t, PICKS), lambda i: (i, 0), memory_space=pltpu.SMEM),
                  pl.BlockSpec((t, d), lambda i: (i, 0)),
                  pl.BlockSpec((t, PICKS), lambda i: (i, 0)),
                  pl.BlockSpec(tbl.shape, lambda i: (0, 0), pipeline_mode=pl.Buffered(1)),
                  pl.BlockSpec(grp.shape, lambda i: (0, 0), pipeline_mode=pl.Buffered(1))],
        out_specs=pl.BlockSpec((t, PICKS), lambda i: (i, 0)),
        out_shape=jax.ShapeDtypeStruct((n, PICKS), F32),
        scratch_shapes=[pltpu.VMEM((t, PICKS), F32), pltpu.VMEM((t * SUBLANES, LANES), F32)]
        + [pltpu.VMEM((SLAB_ROWS, LANES), jnp.uint32)] * (2 * TOKENS_PER_STEP),
        compiler_params=_params(("arbitrary",), VMEM_LIMIT), name="peer_u",
    )(idx, x, gates, tbl, grp)


def _peer_v_body(idx_ref, w_ref, tbl_ref, exp_ref, o_ref, rep_hi, rep_lo, y_scr, *slabs, t):
    diag = _chunk_diag()
    w = w_ref[...]
    w_hi = w.astype(BF16)
    w_lo = (w - w_hi.astype(F32)).astype(BF16)
    rep_hi[...] = jnp.dot(w_hi, exp_ref[...], preferred_element_type=F32)
    rep_lo[...] = jnp.dot(w_lo, exp_ref[...], preferred_element_type=F32)

    def compute_group(tok0, slab_set):
        for s, slab in enumerate(slab_set):
            tok = tok0 + s
            lhs = jnp.concatenate([jnp.where(diag, rep_hi[pl.ds(tok, 1), :], 0.0),
                                   jnp.where(diag, rep_lo[pl.ds(tok, 1), :], 0.0)], axis=0).astype(BF16)
            out = jnp.dot(lhs, pltpu.bitcast(slab[...], BF16), preferred_element_type=F32)
            y_scr[pl.ds(pl.multiple_of(tok * SUBLANES, SUBLANES), SUBLANES), :] = (
                out[:SUBLANES] + out[SUBLANES:])

    _pipelined_tokens(t, idx_ref, tbl_ref, slabs, compute_group)
    for j in range(SUBLANES):
        o_ref[:, j * LANES:(j + 1) * LANES] = y_scr[pl.ds(j, t, stride=SUBLANES), :]


def _peer_v(idx, wts, tbl, t):
    n = idx.shape[0]
    exp = _expand_matrix()
    return pl.pallas_call(
        functools.partial(_peer_v_body, t=t), grid=(n // t,),
        in_specs=[pl.BlockSpec((t, PICKS), lambda i: (i, 0), memory_space=pltpu.SMEM),
                  pl.BlockSpec((t, PICKS), lambda i: (i, 0)),
                  pl.BlockSpec(tbl.shape, lambda i: (0, 0), pipeline_mode=pl.Buffered(1)),
                  pl.BlockSpec(exp.shape, lambda i: (0, 0), pipeline_mode=pl.Buffered(1))],
        out_specs=pl.BlockSpec((t, SUBLANES * LANES), lambda i: (i, 0)),
        out_shape=jax.ShapeDtypeStruct((n, SUBLANES * LANES), F32),
        scratch_shapes=[pltpu.VMEM((t, SLAB_COLS), F32)] * 2 + [pltpu.VMEM((t * SUBLANES, LANES), F32)]
        + [pltpu.VMEM((SLAB_ROWS, LANES), jnp.uint32)] * (2 * TOKENS_PER_STEP),
        compiler_params=_params(("arbitrary",), VMEM_LIMIT), name="peer_v",
    )(idx, wts, tbl, exp)


def _final_body(x_ref, y_ref, g2_ref, lng_ref, lnb_ref, o_ref):
    o_ref[0] = _layer_norm(DEEPNORM_ALPHA * x_ref[0] + g2_ref[0] * y_ref[0], lng_ref[...], lnb_ref[...])


def _final(x1, y, gate2, ln_g, ln_b, tm):
    b, s, d = x1.shape
    row = pl.BlockSpec((1, tm, d), lambda bi, i: (bi, i, 0))
    par = pl.BlockSpec((1, d), lambda bi, i: (0, 0))
    return pl.pallas_call(
        _final_body, grid=(b, s // tm),
        in_specs=[row, row, pl.BlockSpec((1, 1, d), lambda bi, i: (bi, 0, 0)), par, par],
        out_specs=row, out_shape=jax.ShapeDtypeStruct((b, s, d), F32),
        compiler_params=_params(("parallel", "parallel")), name="final",
    )(x1, y, gate2, ln_g, ln_b)


def _tile(n, want):
    while n % want:
        want //= 2
    return want


def kernel(x, c, w_ada, b_ada, w_in, swa_sinks, group_norm_a, group_norm_b, w_out, ln1_g, ln1_b,
           peer_w_q, peer_sub_keys, peer_u, peer_v, ln2_g, ln2_b):
    b, s, d = x.shape
    n = b * s
    assert d == 8 * LANES and s % BLOCK == 0
    depth = w_ada.shape[0]
    tm = _tile(s, 512)
    for l in range(depth):
        ada = _ada(c, w_ada[l], b_ada[l][None])
        shift1, scale1, gate1, shift2, scale2, gate2 = [a[:, None, :] for a in jnp.split(ada, 6, axis=-1)]

        qa, ka, va, qb, kb, vb = _inproj(x, scale1, shift1, w_in[l].astype(BF16), tm)
        oa = _swa(qa, ka, va, swa_sinks[l])
        ob = _sb(qb, kb, vb, _tile(s, 256))

        gn = jnp.concatenate([group_norm_a[l], group_norm_b[l]]).reshape(-1, 1, HEAD_DIM)
        w_o = w_out[l].astype(BF16).reshape(-1, HEAD_DIM, d)
        x1, h2, q = _outproj(oa, ob, x, gate1, scale2, shift2, gn, w_o, ln1_g[l][None], ln1_b[l][None],
                             peer_w_q[l].astype(BF16), _tile(s, 256))

        sk = peer_sub_keys[l].astype(BF16).reshape(2 * PEER_HEADS, PEER_N_KEYS, PEER_KEY_DIM // 2)
        idx, gates = _topk(q.reshape(n, -1), sk, _tile(n, 128))
        tp = _tile(n, 256)
        wts = _peer_u(idx, h2.reshape(n, d), gates, _pack_table(peer_u[l]), tp)
        y = _peer_v(idx, wts, _pack_table(peer_v[l]), tp).reshape(b, s, d)

        x = _final(x1, y, gate2, ln2_g[l][None], ln2_b[l][None], tm)
    return x
```

```python
import functools
import math

import jax
import jax.numpy as jnp
from jax import lax
from jax.experimental import pallas as pl
from jax.experimental.pallas import tpu as pltpu

F32 = jnp.float32
BF16 = jnp.bfloat16
I32 = jnp.int32

HEAD_DIM = 64
SWA_HEADS = 8
SWA_KV_HEADS = 2
SWA_WINDOW = 128
SB_HEADS = 8
BLOCK = 128
SWA_Q = SWA_HEADS * HEAD_DIM
SWA_KV = SWA_KV_HEADS * HEAD_DIM
SB_W = SB_HEADS * HEAD_DIM
PEER_HEADS = 8
PEER_KEY_DIM = 256
PEER_N_KEYS = 128
PEER_TOPK = 16
PICKS = PEER_HEADS * PEER_TOPK
ROW_WORDS = 4
DEPTH = 1
DEEPNORM_ALPHA = (2.0 * DEPTH) ** 0.25
LN_EPS = 1e-5

LANES = 128
SUBLANES = 8
VMEM_LIMIT = 56 * 1024 * 1024


def _params(sem, vmem=None):
    return pltpu.CompilerParams(dimension_semantics=sem, vmem_limit_bytes=vmem)


def _dot_nt(a, b):
    return lax.dot_general(a, b, (((1,), (1,)), ((), ())), preferred_element_type=F32)


def _layer_norm(y, g, b):
    mu = jnp.mean(y, axis=-1, keepdims=True)
    d = y - mu
    var = jnp.mean(d * d, axis=-1, keepdims=True)
    return d * lax.rsqrt(var + LN_EPS) * g + b


def _ada_body(c_ref, w_ref, b_ref, o_ref):
    c = c_ref[...]
    a = c * jax.nn.sigmoid(c)
    o_ref[...] = jnp.dot(a, w_ref[...], preferred_element_type=F32,
                         precision=lax.Precision.HIGHEST) + b_ref[...]


def _ada(c, w, b):
    bsz, d = c.shape
    e = w.shape[1]
    tn = 1024
    return pl.pallas_call(
        _ada_body, grid=(e // tn,),
        in_specs=[pl.BlockSpec((bsz, d), lambda j: (0, 0)),
                  pl.BlockSpec((d, tn), lambda j: (0, j)),
                  pl.BlockSpec((1, tn), lambda j: (0, j))],
        out_specs=pl.BlockSpec((bsz, tn), lambda j: (0, j)),
        out_shape=jax.ShapeDtypeStruct((bsz, e), F32),
        compiler_params=_params(("arbitrary",)), name="ada",
    )(c, w, b)


_IN_SPLITS = (
    (0, SWA_HEADS, 0.125), (SWA_Q, SWA_KV_HEADS, 1.0), (SWA_Q + SWA_KV, SWA_KV_HEADS, 1.0),
    (SWA_Q + 2 * SWA_KV, SB_HEADS, 0.125), (SWA_Q + 2 * SWA_KV + SB_W, SB_HEADS, 1.0),
    (SWA_Q + 2 * SWA_KV + 2 * SB_W, SB_HEADS, 1.0))
IN_COLS = SWA_Q + 2 * SWA_KV + 3 * SB_W
_IN_CHUNK = 256


def _inproj_body(x_ref, sc_ref, sh_ref, w_ref, *out_refs):
    h = (x_ref[0] * (1.0 + sc_ref[0]) + sh_ref[0]).astype(BF16)
    for c in range(IN_COLS // _IN_CHUNK):
        r = jnp.dot(h, w_ref[:, c * _IN_CHUNK:(c + 1) * _IN_CHUNK], preferred_element_type=F32)
        for s in range(_IN_CHUNK // HEAD_DIM):
            col = c * _IN_CHUNK + s * HEAD_DIM
            for (start, heads, scale), o_ref in zip(_IN_SPLITS, out_refs):
                if start <= col < start + heads * HEAD_DIM:
                    piece = r[:, s * HEAD_DIM:(s + 1) * HEAD_DIM]
                    o_ref[0, (col - start) // HEAD_DIM] = (piece * scale).astype(BF16)


def _inproj(x, scale, shift, w_bf16, tm):
    b, s, d = x.shape
    outs = [jax.ShapeDtypeStruct((b, heads, s, HEAD_DIM), BF16) for _, heads, _ in _IN_SPLITS]
    out_specs = [pl.BlockSpec((1, heads, tm, HEAD_DIM), lambda bi, i: (bi, 0, i, 0))
                 for _, heads, _ in _IN_SPLITS]
    return pl.pallas_call(
        _inproj_body, grid=(b, s // tm),
        in_specs=[pl.BlockSpec((1, tm, d), lambda bi, i: (bi, i, 0)),
                  pl.BlockSpec((1, 1, d), lambda bi, i: (bi, 0, 0)),
                  pl.BlockSpec((1, 1, d), lambda bi, i: (bi, 0, 0)),
                  pl.BlockSpec((d, IN_COLS), lambda bi, i: (0, 0))],
        out_specs=out_specs, out_shape=outs,
        compiler_params=_params(("parallel", "parallel"), VMEM_LIMIT), name="inproj",
    )(x, scale, shift, w_bf16)


def _swa_body(sink_ref, q_ref, kp_ref, kc_ref, vp_ref, vc_ref, o_ref):
    j = pl.program_id(1)
    ratio = SWA_HEADS // SWA_KV_HEADS
    rows = ratio * BLOCK
    qi = lax.broadcasted_iota(I32, (rows, BLOCK), 0) & (BLOCK - 1)
    kj = lax.broadcasted_iota(I32, (rows, BLOCK), 1)
    d_cur = (qi - kj).astype(F32)
    d_prev = d_cur + float(BLOCK)
    valid_cur = kj <= qi
    valid_prev = jnp.logical_and(kj > qi, j > 0)
    groups = range(SWA_KV_HEADS)

    def column(values):
        return jnp.concatenate([jnp.full((BLOCK, 1), v, F32) for v in values], axis=0)

    scores = []
    for g in groups:
        q = jnp.concatenate([q_ref[0, g * ratio + r] for r in range(ratio)], axis=0)
        scores.append((_dot_nt(q, kp_ref[0, g]), _dot_nt(q, kc_ref[0, g])))
    probs = []
    for g in groups:
        heads = [g * ratio + r for r in range(ratio)]
        slope = column([2.0 ** (-8.0 * (hd + 1) / SWA_HEADS) for hd in heads])
        sink = column([sink_ref[hd] for hd in heads])
        sp = jnp.where(valid_prev, scores[g][0] - slope * d_prev, -jnp.inf)
        sc = jnp.where(valid_cur, scores[g][1] - slope * d_cur, -jnp.inf)
        m = jnp.maximum(jnp.maximum(jnp.max(sp, axis=1, keepdims=True),
                                    jnp.max(sc, axis=1, keepdims=True)), sink)
        pp = jnp.exp(sp - m)
        pc = jnp.exp(sc - m)
        den = (jnp.sum(pp, axis=1, keepdims=True) + jnp.sum(pc, axis=1, keepdims=True)
               + jnp.exp(sink - m))
        probs.append(((pp / den).astype(BF16), (pc / den).astype(BF16)))
    for g in groups:
        o = (jnp.dot(probs[g][0], vp_ref[0, g], preferred_element_type=F32)
             + jnp.dot(probs[g][1], vc_ref[0, g], preferred_element_type=F32))
        for r in range(ratio):
            o_ref[0, g * ratio + r] = o[r * BLOCK:(r + 1) * BLOCK]


def _swa(qa, ka, va, sinks):
    b, _, s, _ = qa.shape
    nb = s // BLOCK
    cur = lambda bi, j: (bi, 0, j, 0)
    prev = lambda bi, j: (bi, 0, jnp.maximum(j - 1, 0), 0)
    kv_blk = (1, SWA_KV_HEADS, BLOCK, HEAD_DIM)
    return pl.pallas_call(
        _swa_body, grid=(b, nb),
        in_specs=[pl.BlockSpec(memory_space=pltpu.SMEM),
                  pl.BlockSpec((1, SWA_HEADS, BLOCK, HEAD_DIM), cur),
                  pl.BlockSpec(kv_blk, prev), pl.BlockSpec(kv_blk, cur),
                  pl.BlockSpec(kv_blk, prev), pl.BlockSpec(kv_blk, cur)],
        out_specs=pl.BlockSpec((1, SWA_HEADS, BLOCK, HEAD_DIM), cur),
        out_shape=jax.ShapeDtypeStruct((b, SWA_HEADS, s, HEAD_DIM), F32),
        compiler_params=_params(("parallel", "parallel")), name="swa",
    )(sinks, qa, ka, ka, va, va)


SB_HEADS_PER_STEP = 4
LOG2E = 1.4426950408889634


def _sb_body(q_ref, k_ref, v_ref, o_ref, z_scr, carry_scr, *, t):
    i = pl.program_id(2)
    row = lax.broadcasted_iota(I32, (t, t), 0)
    col = lax.broadcasted_iota(I32, (t, t), 1)
    causal = col < row
    tri = (row > col).astype(BF16)
    tri2 = jnp.concatenate([tri, tri], axis=0)
    heads = range(SB_HEADS_PER_STEP)

    def load_scores(j):
        off = pl.multiple_of(j * t, t)
        for hd in heads:
            z_scr[hd] = _dot_nt(q_ref[0, hd], k_ref[0, hd, pl.ds(off, t), :])

    def sweep(j, diagonal):
        off = pl.multiple_of(j * t, t)
        pend = []
        for hd in heads:
            zs = z_scr[hd] * LOG2E
            p = jnp.maximum(zs, 0.0) + jnp.log2(1.0 + jnp.exp2(-jnp.abs(zs)))
            pm = jnp.where(causal, p, 0.0) if diagonal else p
            hi = pm.astype(BF16)
            lo = (pm - hi.astype(F32)).astype(BF16)
            within = jnp.dot(jnp.concatenate([hi, lo], axis=1), tri2, preferred_element_type=F32)
            psum = jnp.broadcast_to(jnp.sum(pm, axis=1, keepdims=True), (t, LANES))
            pend.append((zs - p, within, psum))
        load_scores(jnp.maximum(j - 1, 0))
        for hd in heads:
            base, within, psum = pend[hd]
            e = base - within
            if diagonal:
                a = jnp.where(causal, jnp.exp2(e), 0.0)
                carry_scr[hd] = psum
            else:
                carry = carry_scr[hd]
                a = jnp.exp2(e - jnp.concatenate([carry] * (t // LANES), axis=1))
                carry_scr[hd] = carry + psum
            out = jnp.dot(a.astype(BF16), v_ref[0, hd, pl.ds(off, t), :], preferred_element_type=F32)
            if diagonal:
                o_ref[0, hd] = out
            else:
                o_ref[0, hd] += out

    load_scores(i)
    sweep(i, True)

    def body(jj, _):
        sweep(i - 1 - jj, False)
        return 0

    lax.fori_loop(0, i, body, 0)


def _sb(qb, kb, vb, t):
    b, h, s, _ = qb.shape
    hs = SB_HEADS_PER_STEP
    full = pl.BlockSpec((1, hs, s, HEAD_DIM), lambda bi, hi, i: (bi, hi, 0, 0))
    tile = pl.BlockSpec((1, hs, t, HEAD_DIM), lambda bi, hi, i: (bi, hi, i, 0))
    return pl.pallas_call(
        functools.partial(_sb_body, t=t), grid=(b, h // hs, s // t),
        in_specs=[tile, full, full], out_specs=tile,
        out_shape=jax.ShapeDtypeStruct((b, h, s, HEAD_DIM), F32),
        scratch_shapes=[pltpu.VMEM((hs, t, t), F32), pltpu.VMEM((hs, t, LANES), F32)],
        compiler_params=_params(("parallel", "parallel", "arbitrary"), VMEM_LIMIT), name="sb",
    )(qb, kb, vb)


def _outproj_body(oa_ref, ob_ref, x_ref, g1_ref, sc2_ref, sh2_ref, gn_ref, w_ref, lng_ref, lnb_ref,
                  wq_ref, x1_ref, h2_ref, q_ref):
    def group(o_ref, base, heads):
        ss = None
        for h in range(heads):
            o = o_ref[0, h]
            p = jnp.sum(o * o, axis=1, keepdims=True)
            ss = p if ss is None else ss + p
        inv = lax.rsqrt(ss / float(heads * HEAD_DIM) + LN_EPS)
        mix = None
        for h in range(heads):
            on = (o_ref[0, h] * inv * gn_ref[base + h]).astype(BF16)
            p = jnp.dot(on, w_ref[base + h], preferred_element_type=F32)
            mix = p if mix is None else mix + p
        return mix

    mix = group(oa_ref, 0, SWA_HEADS) + group(ob_ref, SWA_HEADS, SB_HEADS)
    x1 = _layer_norm(DEEPNORM_ALPHA * x_ref[0] + g1_ref[0] * mix, lng_ref[...], lnb_ref[...])
    x1_ref[0] = x1
    h2 = x1 * (1.0 + sc2_ref[0]) + sh2_ref[0]
    h2_ref[0] = h2
    q_ref[0] = jnp.dot(h2.astype(BF16), wq_ref[...], preferred_element_type=F32).astype(BF16)


def _outproj(oa, ob, x, gate1, scale2, shift2, gn, w_out, ln_g, ln_b, wq, tm):
    b, s, d = x.shape
    nq = wq.shape[1]
    nh = SWA_HEADS + SB_HEADS
    row = pl.BlockSpec((1, tm, d), lambda bi, i: (bi, i, 0))
    vec = pl.BlockSpec((1, 1, d), lambda bi, i: (bi, 0, 0))
    par = pl.BlockSpec((1, d), lambda bi, i: (0, 0))
    return pl.pallas_call(
        _outproj_body, grid=(b, s // tm),
        in_specs=[pl.BlockSpec((1, SWA_HEADS, tm, HEAD_DIM), lambda bi, i: (bi, 0, i, 0)),
                  pl.BlockSpec((1, SB_HEADS, tm, HEAD_DIM), lambda bi, i: (bi, 0, i, 0)),
                  row, vec, vec, vec,
                  pl.BlockSpec((nh, 1, HEAD_DIM), lambda bi, i: (0, 0, 0)),
                  pl.BlockSpec((nh, HEAD_DIM, d), lambda bi, i: (0, 0, 0)),
                  par, par,
                  pl.BlockSpec((d, nq), lambda bi, i: (0, 0))],
        out_specs=[row, row, pl.BlockSpec((1, tm, nq), lambda bi, i: (bi, i, 0))],
        out_shape=[jax.ShapeDtypeStruct((b, s, d), F32), jax.ShapeDtypeStruct((b, s, d), F32),
                   jax.ShapeDtypeStruct((b, s, nq), BF16)],
        compiler_params=_params(("parallel", "parallel"), VMEM_LIMIT), name="outproj",
    )(oa, ob, x, gate1, scale2, shift2, gn, w_out, ln_g, ln_b, wq)


def _top_rows(s, ids, k):
    beyond = 1.0e6
    vals, picks = [], []
    for _ in range(k):
        m = jnp.max(s, axis=0, keepdims=True)
        first = jnp.min(jnp.where(s == m, ids, beyond), axis=0, keepdims=True)
        vals.append(m)
        picks.append(first)
        s = jnp.where(ids == first, -jnp.inf, s)
    return jnp.concatenate(vals, axis=0), jnp.concatenate(picks, axis=0)


def _take_rows(table, sel):
    out = jnp.zeros_like(sel)
    for r in range(table.shape[0]):
        out = jnp.where(sel == float(r), table[r:r + 1], out)
    return out


_PAIR_ROWS = PEER_TOPK + 7 * SUBLANES + SUBLANES


def _pair_ids(t):
    r = lax.broadcasted_iota(I32, (_PAIR_ROWS, t), 0)
    mid = r - PEER_TOPK
    mid_id = ((mid >> 3) + 1) * PEER_TOPK + (mid & 7)
    tail_id = (r - (PEER_TOPK + 7 * SUBLANES) + SUBLANES) * PEER_TOPK
    ids = jnp.where(r < PEER_TOPK, r, jnp.where(r < PEER_TOPK + 7 * SUBLANES, mid_id, tail_id))
    return ids.astype(F32)


def _topk_body(q_ref, sk_ref, e_ref, g_ref):
    t = q_ref.shape[0]
    key_ids = lax.broadcasted_iota(I32, (PEER_N_KEYS, t), 0).astype(F32)
    pair_ids = _pair_ids(t)
    half = PEER_KEY_DIM // 2
    rows_out, gates_out = [], []
    for h in range(PEER_HEADS):
        tops = []
        for p in range(2):
            qhp = q_ref[:, (2 * h + p) * half:(2 * h + p + 1) * half]
            sc = _dot_nt(sk_ref[2 * h + p], qhp)
            tops.append(_top_rows(sc, key_ids, PEER_TOPK))
        (s1, i1), (s2, i2) = tops
        cand = jnp.concatenate([s1[0:1] + s2]
                               + [s1[a:a + 1] + s2[0:SUBLANES] for a in range(1, SUBLANES)]
                               + [s1[SUBLANES:] + s2[0:1]], axis=0)
        best, flat = _top_rows(cand, pair_ids, PEER_TOPK)
        a_sel = jnp.floor(flat * (1.0 / PEER_TOPK))
        b_sel = flat - a_sel * PEER_TOPK
        expert = _take_rows(i1, a_sel) * PEER_N_KEYS + _take_rows(i2, b_sel)
        ex = jnp.exp(best - best[0:1])
        rows_out.append(expert * float(ROW_WORDS))
        gates_out.append(ex / jnp.sum(ex, axis=0, keepdims=True))
    e_ref[...] = jnp.concatenate(rows_out, axis=0).T.astype(I32)
    g_ref[...] = jnp.concatenate(gates_out, axis=0).T


def _topk(q, sub_keys_bf16, t):
    n, nq = q.shape
    return pl.pallas_call(
        _topk_body, grid=(n // t,),
        in_specs=[pl.BlockSpec((t, nq), lambda i: (i, 0)),
                  pl.BlockSpec(sub_keys_bf16.shape, lambda i: (0, 0, 0))],
        out_specs=[pl.BlockSpec((t, PICKS), lambda i: (i, 0)), pl.BlockSpec((t, PICKS), lambda i: (i, 0))],
        out_shape=[jax.ShapeDtypeStruct((n, PICKS), I32), jax.ShapeDtypeStruct((n, PICKS), F32)],
        compiler_params=_params(("parallel",), VMEM_LIMIT), name="topk",
    )(q, sub_keys_bf16)


def _pack_table(tbl):
    e, d = tbl.shape
    bits = lax.bitcast_convert_type(tbl.astype(BF16), jnp.uint16).astype(jnp.uint32)
    bits = bits.reshape(e, d // (2 * LANES), 2, LANES)
    words = bits[:, :, 0, :] | (bits[:, :, 1, :] << 16)
    return words.reshape(e * (d // (2 * LANES)), LANES)


def _gelu_tanh(x):
    return 0.5 * x * (1.0 + jnp.tanh(math.sqrt(2.0 / math.pi) * (x + 0.044715 * (x * x * x))))


SLAB_ROWS = PICKS * ROW_WORDS
SLAB_COLS = PICKS * SUBLANES
TOKENS_PER_STEP = 8


def _expand_matrix():
    k = lax.broadcasted_iota(I32, (PICKS, SLAB_COLS), 0)
    c = lax.broadcasted_iota(I32, (PICKS, SLAB_COLS), 1)
    return (c // SUBLANES == k).astype(BF16)


def _gather_slabs(idx_ref, tbl_ref, slab_refs, toks):
    rows = [idx_ref.at[tok] for tok in toks]
    for k in range(PICKS):
        for row, slab_ref in zip(rows, slab_refs):
            src = pl.multiple_of(row[k], ROW_WORDS)
            slab_ref[k * ROW_WORDS:(k + 1) * ROW_WORDS, :] = tbl_ref[pl.ds(src, ROW_WORDS), :]


def _split_rows(a):
    hi = a.astype(BF16).astype(F32)
    return jnp.concatenate([hi, a - hi], axis=0).astype(BF16)


def _chunk_diag():
    sub = lax.broadcasted_iota(I32, (SUBLANES, SLAB_COLS), 0)
    col = lax.broadcasted_iota(I32, (SUBLANES, SLAB_COLS), 1)
    return (col & (SUBLANES - 1)) == sub


def _pipelined_tokens(t, idx_ref, tbl_ref, slabs, compute_group):
    sets = (slabs[:TOKENS_PER_STEP], slabs[TOKENS_PER_STEP:])

    def gather_group(g, slab_set):
        _gather_slabs(idx_ref, tbl_ref, slab_set,
                      [jnp.minimum(g * TOKENS_PER_STEP + s, t - 1) for s in range(TOKENS_PER_STEP)])

    gather_group(0, sets[0])

    def step(p, _):
        gather_group(2 * p + 1, sets[1])
        compute_group(pl.multiple_of(2 * p * TOKENS_PER_STEP, TOKENS_PER_STEP), sets[0])
        gather_group(2 * p + 2, sets[0])
        compute_group(pl.multiple_of((2 * p + 1) * TOKENS_PER_STEP, TOKENS_PER_STEP), sets[1])
        return 0

    lax.fori_loop(0, t // (2 * TOKENS_PER_STEP), step, 0)


def _peer_u_body(idx_ref, x_ref, g_ref, tbl_ref, grp_ref, o_ref, h_scr, x_scr, *slabs, t):
    diag = _chunk_diag()
    for j in range(SUBLANES):
        x_scr[pl.ds(j, t, stride=SUBLANES), :] = x_ref[:, j * LANES:(j + 1) * LANES]

    def compute_group(tok0, slab_set):
        parts = []
        for s, slab in enumerate(slab_set):
            xt = x_scr[pl.ds(pl.multiple_of((tok0 + s) * SUBLANES, SUBLANES), SUBLANES), :]
            r = _dot_nt(_split_rows(xt), pltpu.bitcast(slab[...], BF16))
            parts.append(_split_rows(jnp.where(diag, r[:SUBLANES] + r[SUBLANES:], 0.0)))
        hh = jnp.dot(jnp.concatenate(parts, axis=0), grp_ref[...], preferred_element_type=F32)
        rows = [jnp.sum(hh[2 * SUBLANES * s:2 * SUBLANES * (s + 1)], axis=0, keepdims=True)
                for s in range(len(slab_set))]
        h_scr[pl.ds(tok0, len(slab_set)), :] = jnp.concatenate(rows, axis=0)

    _pipelined_tokens(t, idx_ref, tbl_ref, slabs, compute_group)
    o_ref[...] = g_ref[...] * _gelu_tanh(h_scr[...])


def _peer_u(idx, x, gates, tbl, t):
    n, d = x.shape
    grp = _expand_matrix().T
    return pl.pallas_call(
        functools.partial(_peer_u_body, t=t), grid=(n // t,),
        in_specs=[pl.BlockSpec((t, PICKS), lambda i: (i, 0), memory_space=pltpu.SMEM),
                  pl.BlockSpec((t, d), lambda i: (i, 0)),
                  pl.BlockSpec((t, PICKS), lambda i: (i, 0)),
                  pl.BlockSpec(tbl.shape, lambda i: (0, 0), pipeline_mode=pl.Buffered(1)),
                  pl.BlockSpec(grp.shape, lambda i: (0, 0), pipeline_mode=pl.Buffered(1))],
        out_specs=pl.BlockSpec((t, PICKS), lambda i: (i, 0)),
        out_shape=jax.ShapeDtypeStruct((n, PICKS), F32),
        scratch_shapes=[pltpu.VMEM((t, PICKS), F32), pltpu.VMEM((t * SUBLANES, LANES), F32)]
        + [pltpu.VMEM((SLAB_ROWS, LANES), jnp.uint32)] * (2 * TOKENS_PER_STEP),
        compiler_params=_params(("arbitrary",), VMEM_LIMIT), name="peer_u",
    )(idx, x, gates, tbl, grp)


def _peer_v_body(idx_ref, w_ref, tbl_ref, exp_ref, o_ref, rep_hi, rep_lo, y_scr, *slabs, t):
    diag = _chunk_diag()
    w = w_ref[...]
    w_hi = w.astype(BF16)
    w_lo = (w - w_hi.astype(F32)).astype(BF16)
    rep_hi[...] = jnp.dot(w_hi, exp_ref[...], preferred_element_type=F32)
    rep_lo[...] = jnp.dot(w_lo, exp_ref[...], preferred_element_type=F32)

    def compute_group(tok0, slab_set):
        for s, slab in enumerate(slab_set):
            tok = tok0 + s
            lhs = jnp.concatenate([jnp.where(diag, rep_hi[pl.ds(tok, 1), :], 0.0),
                                   jnp.where(diag, rep_lo[pl.ds(tok, 1), :], 0.0)], axis=0).astype(BF16)
            out = jnp.dot(lhs, pltpu.bitcast(slab[...], BF16), preferred_element_type=F32)
            y_scr[pl.ds(pl.multiple_of(tok * SUBLANES, SUBLANES), SUBLANES), :] = (
                out[:SUBLANES] + out[SUBLANES:])

    _pipelined_tokens(t, idx_ref, tbl_ref, slabs, compute_group)
    for j in range(SUBLANES):
        o_ref[:, j * LANES:(j + 1) * LANES] = y_scr[pl.ds(j, t, stride=SUBLANES), :]


def _peer_v(idx, wts, tbl, t):
    n = idx.shape[0]
    exp = _expand_matrix()
    return pl.pallas_call(
        functools.partial(_peer_v_body, t=t), grid=(n // t,),
        in_specs=[pl.BlockSpec((t, PICKS), lambda i: (i, 0), memory_space=pltpu.SMEM),
                  pl.BlockSpec((t, PICKS), lambda i: (i, 0)),
                  pl.BlockSpec(tbl.shape, lambda i: (0, 0), pipeline_mode=pl.Buffered(1)),
                  pl.BlockSpec(exp.shape, lambda i: (0, 0), pipeline_mode=pl.Buffered(1))],
        out_specs=pl.BlockSpec((t, SUBLANES * LANES), lambda i: (i, 0)),
        out_shape=jax.ShapeDtypeStruct((n, SUBLANES * LANES), F32),
        scratch_shapes=[pltpu.VMEM((t, SLAB_COLS), F32)] * 2 + [pltpu.VMEM((t * SUBLANES, LANES), F32)]
        + [pltpu.VMEM((SLAB_ROWS, LANES), jnp.uint32)] * (2 * TOKENS_PER_STEP),
        compiler_params=_params(("arbitrary",), VMEM_LIMIT), name="peer_v",
    )(idx, wts, tbl, exp)


def _final_body(x_ref, y_ref, g2_ref, lng_ref, lnb_ref, o_ref):
    o_ref[0] = _layer_norm(DEEPNORM_ALPHA * x_ref[0] + g2_ref[0] * y_ref[0], lng_ref[...], lnb_ref[...])


def _final(x1, y, gate2, ln_g, ln_b, tm):
    b, s, d = x1.shape
    row = pl.BlockSpec((1, tm, d), lambda bi, i: (bi, i, 0))
    par = pl.BlockSpec((1, d), lambda bi, i: (0, 0))
    return pl.pallas_call(
        _final_body, grid=(b, s // tm),
        in_specs=[row, row, pl.BlockSpec((1, 1, d), lambda bi, i: (bi, 0, 0)), par, par],
        out_specs=row, out_shape=jax.ShapeDtypeStruct((b, s, d), F32),
        compiler_params=_params(("parallel", "parallel")), name="final",
    )(x1, y, gate2, ln_g, ln_b)


def _tile(n, want):
    while n % want:
        want //= 2
    return want


def kernel(x, c, w_ada, b_ada, w_in, swa_sinks, group_norm_a, group_norm_b, w_out, ln1_g, ln1_b,
           peer_w_q, peer_sub_keys, peer_u, peer_v, ln2_g, ln2_b):
    b, s, d = x.shape
    n = b * s
    assert d == 8 * LANES and s % BLOCK == 0
    depth = w_ada.shape[0]
    tm = _tile(s, 512)
    for l in range(depth):
        ada = _ada(c, w_ada[l], b_ada[l][None])
        shift1, scale1, gate1, shift2, scale2, gate2 = [a[:, None, :] for a in jnp.split(ada, 6, axis=-1)]

        qa, ka, va, qb, kb, vb = _inproj(x, scale1, shift1, w_in[l].astype(BF16), tm)
        oa = _swa(qa, ka, va, swa_sinks[l])
        ob = _sb(qb, kb, vb, _tile(s, 256))

        gn = jnp.concatenate([group_norm_a[l], group_norm_b[l]]).reshape(-1, 1, HEAD_DIM)
        w_o = w_out[l].astype(BF16).reshape(-1, HEAD_DIM, d)
        x1, h2, q = _outproj(oa, ob, x, gate1, scale2, shift2, gn, w_o, ln1_g[l][None], ln1_b[l][None],
                             peer_w_q[l].astype(BF16), _tile(s, 256))

        sk = peer_sub_keys[l].astype(BF16).reshape(2 * PEER_HEADS, PEER_N_KEYS, PEER_KEY_DIM // 2)
        idx, gates = _topk(q.reshape(n, -1), sk, _tile(n, 128))
        tp = _tile(n, 256)
        wts = _peer_u(idx, h2.reshape(n, d), gates, _pack_table(peer_u[l]), tp)
        y = _peer_v(idx, wts, _pack_table(peer_v[l]), tp).reshape(b, s, d)

        x = _final(x1, y, gate2, ln2_g[l][None], ln2_b[l][None], tm)
    return x
```

```python
import functools
import math

import jax
import jax.numpy as jnp
from jax import lax
from jax.experimental import pallas as pl
from jax.experimental.pallas import tpu as pltpu

F32 = jnp.float32
BF16 = jnp.bfloat16
I32 = jnp.int32

HEAD_DIM = 64
SWA_HEADS = 8
SWA_KV_HEADS = 2
SWA_WINDOW = 128
SB_HEADS = 8
BLOCK = 128
SWA_Q = SWA_HEADS * HEAD_DIM
SWA_KV = SWA_KV_HEADS * HEAD_DIM
SB_W = SB_HEADS * HEAD_DIM
PEER_HEADS = 8
PEER_KEY_DIM = 256
PEER_N_KEYS = 128
PEER_TOPK = 16
PICKS = PEER_HEADS * PEER_TOPK
ROW_WORDS = 4
DEPTH = 1
DEEPNORM_ALPHA = (2.0 * DEPTH) ** 0.25
LN_EPS = 1e-5

LANES = 128
SUBLANES = 8
VMEM_LIMIT = 56 * 1024 * 1024


def _params(sem, vmem=None):
    return pltpu.CompilerParams(dimension_semantics=sem, vmem_limit_bytes=vmem)


def _dot_nt(a, b):
    return lax.dot_general(a, b, (((1,), (1,)), ((), ())), preferred_element_type=F32)


def _layer_norm(y, g, b):
    mu = jnp.mean(y, axis=-1, keepdims=True)
    d = y - mu
    var = jnp.mean(d * d, axis=-1, keepdims=True)
    return d * lax.rsqrt(var + LN_EPS) * g + b


def _ada_body(c_ref, w_ref, b_ref, o_ref):
    c = c_ref[...]
    a = c * jax.nn.sigmoid(c)
    o_ref[...] = jnp.dot(a, w_ref[...], preferred_element_type=F32,
                         precision=lax.Precision.HIGHEST) + b_ref[...]


def _ada(c, w, b):
    bsz, d = c.shape
    e = w.shape[1]
    tn = 1024
    return pl.pallas_call(
        _ada_body, grid=(e // tn,),
        in_specs=[pl.BlockSpec((bsz, d), lambda j: (0, 0)),
                  pl.BlockSpec((d, tn), lambda j: (0, j)),
                  pl.BlockSpec((1, tn), lambda j: (0, j))],
        out_specs=pl.BlockSpec((bsz, tn), lambda j: (0, j)),
        out_shape=jax.ShapeDtypeStruct((bsz, e), F32),
        compiler_params=_params(("arbitrary",)), name="ada",
    )(c, w, b)


LOG2E = 1.4426950408889634
_IN_SPLITS = (
    (0, SWA_HEADS, 0.125), (SWA_Q, SWA_KV_HEADS, 1.0), (SWA_Q + SWA_KV, SWA_KV_HEADS, 1.0),
    (SWA_Q + 2 * SWA_KV, SB_HEADS, 0.125 * LOG2E), (SWA_Q + 2 * SWA_KV + SB_W, SB_HEADS, 1.0),
    (SWA_Q + 2 * SWA_KV + 2 * SB_W, SB_HEADS, 1.0))
IN_COLS = SWA_Q + 2 * SWA_KV + 3 * SB_W
_IN_CHUNK = 256


def _inproj_body(x_ref, sc_ref, sh_ref, w_ref, *out_refs):
    h = (x_ref[0] * (1.0 + sc_ref[0]) + sh_ref[0]).astype(BF16)
    for c in range(IN_COLS // _IN_CHUNK):
        r = jnp.dot(h, w_ref[:, c * _IN_CHUNK:(c + 1) * _IN_CHUNK], preferred_element_type=F32)
        for s in range(_IN_CHUNK // HEAD_DIM):
            col = c * _IN_CHUNK + s * HEAD_DIM
            for (start, heads, scale), o_ref in zip(_IN_SPLITS, out_refs):
                if start <= col < start + heads * HEAD_DIM:
                    piece = r[:, s * HEAD_DIM:(s + 1) * HEAD_DIM]
                    o_ref[0, (col - start) // HEAD_DIM] = (piece * scale).astype(BF16)


def _inproj(x, scale, shift, w_bf16, tm):
    b, s, d = x.shape
    outs = [jax.ShapeDtypeStruct((b, heads, s, HEAD_DIM), BF16) for _, heads, _ in _IN_SPLITS]
    out_specs = [pl.BlockSpec((1, heads, tm, HEAD_DIM), lambda bi, i: (bi, 0, i, 0))
                 for _, heads, _ in _IN_SPLITS]
    return pl.pallas_call(
        _inproj_body, grid=(b, s // tm),
        in_specs=[pl.BlockSpec((1, tm, d), lambda bi, i: (bi, i, 0)),
                  pl.BlockSpec((1, 1, d), lambda bi, i: (bi, 0, 0)),
                  pl.BlockSpec((1, 1, d), lambda bi, i: (bi, 0, 0)),
                  pl.BlockSpec((d, IN_COLS), lambda bi, i: (0, 0))],
        out_specs=out_specs, out_shape=outs,
        compiler_params=_params(("parallel", "parallel"), VMEM_LIMIT), name="inproj",
    )(x, scale, shift, w_bf16)


def _swa_body(sink_ref, q_ref, kp_ref, kc_ref, vp_ref, vc_ref, o_ref):
    j = pl.program_id(1)
    ratio = SWA_HEADS // SWA_KV_HEADS
    rows = ratio * BLOCK
    qi = lax.broadcasted_iota(I32, (rows, BLOCK), 0) & (BLOCK - 1)
    kj = lax.broadcasted_iota(I32, (rows, BLOCK), 1)
    d_cur = (qi - kj).astype(F32)
    d_prev = d_cur + float(BLOCK)
    valid_cur = kj <= qi
    valid_prev = jnp.logical_and(kj > qi, j > 0)
    groups = range(SWA_KV_HEADS)

    def column(values):
        return jnp.concatenate([jnp.full((BLOCK, 1), v, F32) for v in values], axis=0)

    scores = []
    for g in groups:
        q = jnp.concatenate([q_ref[0, g * ratio + r] for r in range(ratio)], axis=0)
        scores.append((_dot_nt(q, kp_ref[0, g]), _dot_nt(q, kc_ref[0, g])))
    probs = []
    for g in groups:
        heads = [g * ratio + r for r in range(ratio)]
        slope = column([2.0 ** (-8.0 * (hd + 1) / SWA_HEADS) for hd in heads])
        sink = column([sink_ref[hd] for hd in heads])
        sp = jnp.where(valid_prev, scores[g][0] - slope * d_prev, -jnp.inf)
        sc = jnp.where(valid_cur, scores[g][1] - slope * d_cur, -jnp.inf)
        m = jnp.maximum(jnp.maximum(jnp.max(sp, axis=1, keepdims=True),
                                    jnp.max(sc, axis=1, keepdims=True)), sink)
        pp = jnp.exp(sp - m)
        pc = jnp.exp(sc - m)
        den = (jnp.sum(pp, axis=1, keepdims=True) + jnp.sum(pc, axis=1, keepdims=True)
               + jnp.exp(sink - m))
        probs.append(((pp / den).astype(BF16), (pc / den).astype(BF16)))
    for g in groups:
        o = (jnp.dot(probs[g][0], vp_ref[0, g], preferred_element_type=F32)
             + jnp.dot(probs[g][1], vc_ref[0, g], preferred_element_type=F32))
        for r in range(ratio):
            o_ref[0, g * ratio + r] = o[r * BLOCK:(r + 1) * BLOCK]


def _swa(qa, ka, va, sinks):
    b, _, s, _ = qa.shape
    nb = s // BLOCK
    cur = lambda bi, j: (bi, 0, j, 0)
    prev = lambda bi, j: (bi, 0, jnp.maximum(j - 1, 0), 0)
    kv_blk = (1, SWA_KV_HEADS, BLOCK, HEAD_DIM)
    return pl.pallas_call(
        _swa_body, grid=(b, nb),
        in_specs=[pl.BlockSpec(memory_space=pltpu.SMEM),
                  pl.BlockSpec((1, SWA_HEADS, BLOCK, HEAD_DIM), cur),
                  pl.BlockSpec(kv_blk, prev), pl.BlockSpec(kv_blk, cur),
                  pl.BlockSpec(kv_blk, prev), pl.BlockSpec(kv_blk, cur)],
        out_specs=pl.BlockSpec((1, SWA_HEADS, BLOCK, HEAD_DIM), cur),
        out_shape=jax.ShapeDtypeStruct((b, SWA_HEADS, s, HEAD_DIM), F32),
        compiler_params=_params(("parallel", "parallel")), name="swa",
    )(sinks, qa, ka, ka, va, va)


SB_HEADS_PER_STEP = 4


def _sb_body(q_ref, k_ref, v_ref, o_ref, z_scr, carry_scr, *, t):
    i = pl.program_id(2)
    row = lax.broadcasted_iota(I32, (t, t), 0)
    col = lax.broadcasted_iota(I32, (t, t), 1)
    causal = col < row
    tri = (row > col).astype(BF16)
    tri2 = jnp.concatenate([tri, tri], axis=0)
    heads = range(SB_HEADS_PER_STEP)

    def load_scores(j):
        off = pl.multiple_of(j * t, t)
        for hd in heads:
            z_scr[hd] = _dot_nt(q_ref[0, hd], k_ref[0, hd, pl.ds(off, t), :])

    def sweep(j, diagonal):
        off = pl.multiple_of(j * t, t)
        pend = []
        for hd in heads:
            zs = z_scr[hd]
            p = jnp.maximum(zs, 0.0) + jnp.log2(1.0 + jnp.exp2(-jnp.abs(zs)))
            pm = jnp.where(causal, p, 0.0) if diagonal else p
            hi = pm.astype(BF16)
            lo = (pm - hi.astype(F32)).astype(BF16)
            within = jnp.dot(jnp.concatenate([hi, lo], axis=1), tri2, preferred_element_type=F32)
            psum = jnp.broadcast_to(jnp.sum(pm, axis=1, keepdims=True), (t, LANES))
            pend.append((zs - p, within, psum))
        load_scores(jnp.maximum(j - 1, 0))
        for hd in heads:
            base, within, psum = pend[hd]
            e = base - within
            if diagonal:
                a = jnp.where(causal, jnp.exp2(e), 0.0)
                carry_scr[hd] = psum
            else:
                carry = carry_scr[hd]
                a = jnp.exp2(e - jnp.concatenate([carry] * (t // LANES), axis=1))
                carry_scr[hd] = carry + psum
            out = jnp.dot(a.astype(BF16), v_ref[0, hd, pl.ds(off, t), :], preferred_element_type=F32)
            if diagonal:
                o_ref[0, hd] = out
            else:
                o_ref[0, hd] += out

    load_scores(i)
    sweep(i, True)

    def body(jj, _):
        sweep(i - 1 - jj, False)
        return 0

    lax.fori_loop(0, i, body, 0)


def _sb(qb, kb, vb, t):
    b, h, s, _ = qb.shape
    hs = SB_HEADS_PER_STEP
    full = pl.BlockSpec((1, hs, s, HEAD_DIM), lambda bi, hi, i: (bi, hi, 0, 0))
    tile = pl.BlockSpec((1, hs, t, HEAD_DIM), lambda bi, hi, i: (bi, hi, i, 0))
    return pl.pallas_call(
        functools.partial(_sb_body, t=t), grid=(b, h // hs, s // t),
        in_specs=[tile, full, full], out_specs=tile,
        out_shape=jax.ShapeDtypeStruct((b, h, s, HEAD_DIM), F32),
        scratch_shapes=[pltpu.VMEM((hs, t, t), F32), pltpu.VMEM((hs, t, LANES), F32)],
        compiler_params=_params(("parallel", "parallel", "arbitrary"), VMEM_LIMIT), name="sb",
    )(qb, kb, vb)


def _outproj_body(oa_ref, ob_ref, x_ref, g1_ref, sc2_ref, sh2_ref, gn_ref, w_ref, lng_ref, lnb_ref,
                  wq_ref, x1_ref, h2_ref, q_ref):
    def group(o_ref, base, heads):
        ss = None
        for h in range(heads):
            o = o_ref[0, h]
            p = jnp.sum(o * o, axis=1, keepdims=True)
            ss = p if ss is None else ss + p
        inv = lax.rsqrt(ss / float(heads * HEAD_DIM) + LN_EPS)
        mix = None
        for h in range(heads):
            on = (o_ref[0, h] * inv * gn_ref[base + h]).astype(BF16)
            p = jnp.dot(on, w_ref[base + h], preferred_element_type=F32)
            mix = p if mix is None else mix + p
        return mix

    mix = group(oa_ref, 0, SWA_HEADS) + group(ob_ref, SWA_HEADS, SB_HEADS)
    x1 = _layer_norm(DEEPNORM_ALPHA * x_ref[0] + g1_ref[0] * mix, lng_ref[...], lnb_ref[...])
    x1_ref[0] = x1
    h2 = x1 * (1.0 + sc2_ref[0]) + sh2_ref[0]
    h2_ref[0] = h2
    q_ref[0] = jnp.dot(h2.astype(BF16), wq_ref[...], preferred_element_type=F32).astype(BF16)


def _outproj(oa, ob, x, gate1, scale2, shift2, gn, w_out, ln_g, ln_b, wq, tm):
    b, s, d = x.shape
    nq = wq.shape[1]
    nh = SWA_HEADS + SB_HEADS
    row = pl.BlockSpec((1, tm, d), lambda bi, i: (bi, i, 0))
    vec = pl.BlockSpec((1, 1, d), lambda bi, i: (bi, 0, 0))
    par = pl.BlockSpec((1, d), lambda bi, i: (0, 0))
    return pl.pallas_call(
        _outproj_body, grid=(b, s // tm),
        in_specs=[pl.BlockSpec((1, SWA_HEADS, tm, HEAD_DIM), lambda bi, i: (bi, 0, i, 0)),
                  pl.BlockSpec((1, SB_HEADS, tm, HEAD_DIM), lambda bi, i: (bi, 0, i, 0)),
                  row, vec, vec, vec,
                  pl.BlockSpec((nh, 1, HEAD_DIM), lambda bi, i: (0, 0, 0)),
                  pl.BlockSpec((nh, HEAD_DIM, d), lambda bi, i: (0, 0, 0)),
                  par, par,
                  pl.BlockSpec((d, nq), lambda bi, i: (0, 0))],
        out_specs=[row, row, pl.BlockSpec((1, tm, nq), lambda bi, i: (bi, i, 0))],
        out_shape=[jax.ShapeDtypeStruct((b, s, d), F32), jax.ShapeDtypeStruct((b, s, d), F32),
                   jax.ShapeDtypeStruct((b, s, nq), BF16)],
        compiler_params=_params(("parallel", "parallel"), VMEM_LIMIT), name="outproj",
    )(oa, ob, x, gate1, scale2, shift2, gn, w_out, ln_g, ln_b, wq)


def _top_rows(s, ids, k):
    beyond = 1.0e6
    vals, picks = [], []
    for _ in range(k):
        m = jnp.max(s, axis=0, keepdims=True)
        first = jnp.min(jnp.where(s == m, ids, beyond), axis=0, keepdims=True)
        vals.append(m)
        picks.append(first)
        s = jnp.where(ids == first, -jnp.inf, s)
    return jnp.concatenate(vals, axis=0), jnp.concatenate(picks, axis=0)


def _take_rows(table, sel):
    out = jnp.zeros_like(sel)
    for r in range(table.shape[0]):
        out = jnp.where(sel == float(r), table[r:r + 1], out)
    return out


_PAIR_ROWS = PEER_TOPK + 7 * SUBLANES + SUBLANES


def _pair_ids(t):
    r = lax.broadcasted_iota(I32, (_PAIR_ROWS, t), 0)
    mid = r - PEER_TOPK
    mid_id = ((mid >> 3) + 1) * PEER_TOPK + (mid & 7)
    tail_id = (r - (PEER_TOPK + 7 * SUBLANES) + SUBLANES) * PEER_TOPK
    ids = jnp.where(r < PEER_TOPK, r, jnp.where(r < PEER_TOPK + 7 * SUBLANES, mid_id, tail_id))
    return ids.astype(F32)


def _topk_body(q_ref, sk_ref, e_ref, g_ref):
    t = q_ref.shape[0]
    key_ids = lax.broadcasted_iota(I32, (PEER_N_KEYS, t), 0).astype(F32)
    pair_ids = _pair_ids(t)
    half = PEER_KEY_DIM // 2
    rows_out, gates_out = [], []
    for h in range(PEER_HEADS):
        tops = []
        for p in range(2):
            qhp = q_ref[:, (2 * h + p) * half:(2 * h + p + 1) * half]
            sc = _dot_nt(sk_ref[2 * h + p], qhp)
            tops.append(_top_rows(sc, key_ids, PEER_TOPK))
        (s1, i1), (s2, i2) = tops
        cand = jnp.concatenate([s1[0:1] + s2]
                               + [s1[a:a + 1] + s2[0:SUBLANES] for a in range(1, SUBLANES)]
                               + [s1[SUBLANES:] + s2[0:1]], axis=0)
        best, flat = _top_rows(cand, pair_ids, PEER_TOPK)
        a_sel = jnp.floor(flat * (1.0 / PEER_TOPK))
        b_sel = flat - a_sel * PEER_TOPK
        expert = _take_rows(i1, a_sel) * PEER_N_KEYS + _take_rows(i2, b_sel)
        ex = jnp.exp(best - best[0:1])
        rows_out.append(expert * float(ROW_WORDS))
        gates_out.append(ex / jnp.sum(ex, axis=0, keepdims=True))
    e_ref[...] = jnp.concatenate(rows_out, axis=0).T.astype(I32)
    g_ref[...] = jnp.concatenate(gates_out, axis=0).T


def _topk(q, sub_keys_bf16, t):
    n, nq = q.shape
    return pl.pallas_call(
        _topk_body, grid=(n // t,),
        in_specs=[pl.BlockSpec((t, nq), lambda i: (i, 0)),
                  pl.BlockSpec(sub_keys_bf16.shape, lambda i: (0, 0, 0))],
        out_specs=[pl.BlockSpec((t, PICKS), lambda i: (i, 0)), pl.BlockSpec((t, PICKS), lambda i: (i, 0))],
        out_shape=[jax.ShapeDtypeStruct((n, PICKS), I32), jax.ShapeDtypeStruct((n, PICKS), F32)],
        compiler_params=_params(("parallel",), VMEM_LIMIT), name="topk",
    )(q, sub_keys_bf16)


def _pack_table(tbl):
    e, d = tbl.shape
    bits = lax.bitcast_convert_type(tbl.astype(BF16), jnp.uint16).astype(jnp.uint32)
    bits = bits.reshape(e, d // (2 * LANES), 2, LANES)
    words = bits[:, :, 0, :] | (bits[:, :, 1, :] << 16)
    return words.reshape(e * (d // (2 * LANES)), LANES)


def _gelu_tanh(x):
    return 0.5 * x * (1.0 + jnp.tanh(math.sqrt(2.0 / math.pi) * (x + 0.044715 * (x * x * x))))


SLAB_ROWS = PICKS * ROW_WORDS
SLAB_COLS = PICKS * SUBLANES
TOKENS_PER_STEP = 8


def _expand_matrix():
    k = lax.broadcasted_iota(I32, (PICKS, SLAB_COLS), 0)
    c = lax.broadcasted_iota(I32, (PICKS, SLAB_COLS), 1)
    return (c // SUBLANES == k).astype(BF16)


def _gather_slabs(idx_ref, tbl_ref, slab_refs, toks):
    rows = [idx_ref.at[tok] for tok in toks]
    for k in range(PICKS):
        for row, slab_ref in zip(rows, slab_refs):
            src = pl.multiple_of(row[k], ROW_WORDS)
            slab_ref[k * ROW_WORDS:(k + 1) * ROW_WORDS, :] = tbl_ref[pl.ds(src, ROW_WORDS), :]


def _split_rows(a):
    hi = a.astype(BF16).astype(F32)
    return jnp.concatenate([hi, a - hi], axis=0).astype(BF16)


def _chunk_diag():
    sub = lax.broadcasted_iota(I32, (SUBLANES, SLAB_COLS), 0)
    col = lax.broadcasted_iota(I32, (SUBLANES, SLAB_COLS), 1)
    return (col & (SUBLANES - 1)) == sub


def _pipelined_tokens(t, idx_ref, tbl_ref, slabs, compute_group):
    sets = (slabs[:TOKENS_PER_STEP], slabs[TOKENS_PER_STEP:])

    def gather_group(g, slab_set):
        _gather_slabs(idx_ref, tbl_ref, slab_set,
                      [jnp.minimum(g * TOKENS_PER_STEP + s, t - 1) for s in range(TOKENS_PER_STEP)])

    gather_group(0, sets[0])

    def step(p, _):
        gather_group(2 * p + 1, sets[1])
        compute_group(pl.multiple_of(2 * p * TOKENS_PER_STEP, TOKENS_PER_STEP), sets[0])
        gather_group(2 * p + 2, sets[0])
        compute_group(pl.multiple_of((2 * p + 1) * TOKENS_PER_STEP, TOKENS_PER_STEP), sets[1])
        return 0

    lax.fori_loop(0, t // (2 * TOKENS_PER_STEP), step, 0)


def _peer_u_body(idx_ref, x_ref, g_ref, tbl_ref, grp_ref, o_ref, h_scr, x_scr, *slabs, t):
    diag = _chunk_diag()
    for j in range(SUBLANES):
        x_scr[pl.ds(j, t, stride=SUBLANES), :] = x_ref[:, j * LANES:(j + 1) * LANES]

    def compute_group(tok0, slab_set):
        parts = []
        for s, slab in enumerate(slab_set):
            xt = x_scr[pl.ds(pl.multiple_of((tok0 + s) * SUBLANES, SUBLANES), SUBLANES), :]
            r = _dot_nt(_split_rows(xt), pltpu.bitcast(slab[...], BF16))
            parts.append(_split_rows(jnp.where(diag, r[:SUBLANES] + r[SUBLANES:], 0.0)))
        hh = jnp.dot(jnp.concatenate(parts, axis=0), grp_ref[...], preferred_element_type=F32)
        rows = [jnp.sum(hh[2 * SUBLANES * s:2 * SUBLANES * (s + 1)], axis=0, keepdims=True)
                for s in range(len(slab_set))]
        h_scr[pl.ds(tok0, len(slab_set)), :] = jnp.concatenate(rows, axis=0)

    _pipelined_tokens(t, idx_ref, tbl_ref, slabs, compute_group)
    o_ref[...] = g_ref[...] * _gelu_tanh(h_scr[...])


def _peer_u(idx, x, gates, tbl, t):
    n, d = x.shape
    grp = _expand_matrix().T
    return pl.pallas_call(
        functools.partial(_peer_u_body, t=t), grid=(n // t,),
        in_specs=[pl.BlockSpec((t, PICKS), lambda i: (i, 0), memory_space=pltpu.SMEM),
                  pl.BlockSpec((t, d), lambda i: (i, 0)),
                  pl.BlockSpec((t, PICKS), lambda i: (i, 0)),
                  pl.BlockSpec(tbl.shape, lambda i: (0, 0), pipeline_mode=pl.Buffered(1)),
                  pl.BlockSpec(grp.shape, lambda i: (0, 0), pipeline_mode=pl.Buffered(1))],
        out_specs=pl.BlockSpec((t, PICKS), lambda i: (i, 0)),
        out_shape=jax.ShapeDtypeStruct((n, PICKS), F32),
        scratch_shapes=[pltpu.VMEM((t, PICKS), F32), pltpu.VMEM((t * SUBLANES, LANES), F32)]
        + [pltpu.VMEM((SLAB_ROWS, LANES), jnp.uint32)] * (2 * TOKENS_PER_STEP),
        compiler_params=_params(("arbitrary",), VMEM_LIMIT), name="peer_u",
    )(idx, x, gates, tbl, grp)


def _peer_v_body(idx_ref, w_ref, tbl_ref, exp_ref, o_ref, rep_hi, rep_lo, y_scr, *slabs, t):
    diag = _chunk_diag()
    w = w_ref[...]
    w_hi = w.astype(BF16)
    w_lo = (w - w_hi.astype(F32)).astype(BF16)
    rep_hi[...] = jnp.dot(w_hi, exp_ref[...], preferred_element_type=F32)
    rep_lo[...] = jnp.dot(w_lo, exp_ref[...], preferred_element_type=F32)

    def compute_group(tok0, slab_set):
        for s, slab in enumerate(slab_set):
            tok = tok0 + s
            lhs = jnp.concatenate([jnp.where(diag, rep_hi[pl.ds(tok, 1), :], 0.0),
                                   jnp.where(diag, rep_lo[pl.ds(tok, 1), :], 0.0)], axis=0).astype(BF16)
            out = jnp.dot(lhs, pltpu.bitcast(slab[...], BF16), preferred_element_type=F32)
            y_scr[pl.ds(pl.multiple_of(tok * SUBLANES, SUBLANES), SUBLANES), :] = (
                out[:SUBLANES] + out[SUBLANES:])

    _pipelined_tokens(t, idx_ref, tbl_ref, slabs, compute_group)
    for j in range(SUBLANES):
        o_ref[:, j * LANES:(j + 1) * LANES] = y_scr[pl.ds(j, t, stride=SUBLANES), :]


def _peer_v(idx, wts, tbl, t):
    n = idx.shape[0]
    exp = _expand_matrix()
    return pl.pallas_call(
        functools.partial(_peer_v_body, t=t), grid=(n // t,),
        in_specs=[pl.BlockSpec((t, PICKS), lambda i: (i, 0), memory_space=pltpu.SMEM),
                  pl.BlockSpec((t, PICKS), lambda i: (i, 0)),
                  pl.BlockSpec(tbl.shape, lambda i: (0, 0), pipeline_mode=pl.Buffered(1)),
                  pl.BlockSpec(exp.shape, lambda i: (0, 0), pipeline_mode=pl.Buffered(1))],
        out_specs=pl.BlockSpec((t, SUBLANES * LANES), lambda i: (i, 0)),
        out_shape=jax.ShapeDtypeStruct((n, SUBLANES * LANES), F32),
        scratch_shapes=[pltpu.VMEM((t, SLAB_COLS), F32)] * 2 + [pltpu.VMEM((t * SUBLANES, LANES), F32)]
        + [pltpu.VMEM((SLAB_ROWS, LANES), jnp.uint32)] * (2 * TOKENS_PER_STEP),
        compiler_params=_params(("arbitrary",), VMEM_LIMIT), name="peer_v",
    )(idx, wts, tbl, exp)


def _final_body(x_ref, y_ref, g2_ref, lng_ref, lnb_ref, o_ref):
    o_ref[0] = _layer_norm(DEEPNORM_ALPHA * x_ref[0] + g2_ref[0] * y_ref[0], lng_ref[...], lnb_ref[...])


def _final(x1, y, gate2, ln_g, ln_b, tm):
    b, s, d = x1.shape
    row = pl.BlockSpec((1, tm, d), lambda bi, i: (bi, i, 0))
    par = pl.BlockSpec((1, d), lambda bi, i: (0, 0))
    return pl.pallas_call(
        _final_body, grid=(b, s // tm),
        in_specs=[row, row, pl.BlockSpec((1, 1, d), lambda bi, i: (bi, 0, 0)), par, par],
        out_specs=row, out_shape=jax.ShapeDtypeStruct((b, s, d), F32),
        compiler_params=_params(("parallel", "parallel")), name="final",
    )(x1, y, gate2, ln_g, ln_b)


def _tile(n, want):
    while n % want:
        want //= 2
    return want


def kernel(x, c, w_ada, b_ada, w_in, swa_sinks, group_norm_a, group_norm_b, w_out, ln1_g, ln1_b,
           peer_w_q, peer_sub_keys, peer_u, peer_v, ln2_g, ln2_b):
    b, s, d = x.shape
    n = b * s
    assert d == 8 * LANES and s % BLOCK == 0
    depth = w_ada.shape[0]
    tm = _tile(s, 512)
    for l in range(depth):
        ada = _ada(c, w_ada[l], b_ada[l][None])
        shift1, scale1, gate1, shift2, scale2, gate2 = [a[:, None, :] for a in jnp.split(ada, 6, axis=-1)]

        qa, ka, va, qb, kb, vb = _inproj(x, scale1, shift1, w_in[l].astype(BF16), tm)
        oa = _swa(qa, ka, va, swa_sinks[l])
        ob = _sb(qb, kb, vb, _tile(s, 256))

        gn = jnp.concatenate([group_norm_a[l], group_norm_b[l]]).reshape(-1, 1, HEAD_DIM)
        w_o = w_out[l].astype(BF16).reshape(-1, HEAD_DIM, d)
        x1, h2, q = _outproj(oa, ob, x, gate1, scale2, shift2, gn, w_o, ln1_g[l][None], ln1_b[l][None],
                             peer_w_q[l].astype(BF16), _tile(s, 256))

        sk = peer_sub_keys[l].astype(BF16).reshape(2 * PEER_HEADS, PEER_N_KEYS, PEER_KEY_DIM // 2)
        idx, gates = _topk(q.reshape(n, -1), sk, _tile(n, 128))
        tp = _tile(n, 512)
        wts = _peer_u(idx, h2.reshape(n, d), gates, _pack_table(peer_u[l]), tp)
        y = _peer_v(idx, wts, _pack_table(peer_v[l]), tp).reshape(b, s, d)

        x = _final(x1, y, gate2, ln2_g[l][None], ln2_b[l][None], tm)
    return x
```
